```python
import math
import functools
import jax
import jax.numpy as jnp
from jax import lax
import numpy as np

D_MODEL = 1024
BATCH = 8
SEQ = 4096
DEPTH = 2
DEC_BATCH = 32
DEC_SEQ = 1
PAST_LEN = 16384
PAGE_SIZE = 128

N_A_LAYERS = DEPTH // 2
N_B_LAYERS = DEPTH - N_A_LAYERS
A_HEADS = 8
A_HEAD_DIM = D_MODEL // A_HEADS
A_CHUNK = 128
B_HEADS = 8
B_HEAD_DIM = D_MODEL // B_HEADS
Q_BLOCK = 128
PLE_DIM = 256
PEER_HEADS = 8
PEER_NKEYS = 128
PEER_EXPERTS = PEER_NKEYS * PEER_NKEYS
PEER_KEY_DIM = 256
PEER_TOPK = 16
PEER_TOKEN_BLOCK = 256
ALPHA = (2.0 * DEPTH) ** 0.25
BETA = (8.0 * DEPTH) ** -0.25
LN_EPS = 1e-5
HEAD_NORM_EPS = 1e-6
F32 = jnp.float32

kernel_name = 'yoco_mlstm_fox_peer_decoder_step'


def layer_norm(x, g, b):
    xf = x.astype(F32)
    mu = jnp.mean(xf, axis=-1, keepdims=True)
    var = jnp.mean(jnp.square(xf - mu), axis=-1, keepdims=True)
    return ((xf - mu) * lax.rsqrt(var + LN_EPS) * g.astype(F32) + b.astype(F32)).astype(x.dtype)


def mlstm_chunk_step(carry, inp):
    C, n, m = carry
    q, k, v, li, lf = inp
    L = q.shape[2]
    b = jnp.cumsum(lf, axis=-1)
    causal = jnp.tril(jnp.ones((L, L), dtype=bool))
    dmat = jnp.where(causal, b[..., :, None] - b[..., None, :] + li[..., None, :], -jnp.inf)
    inter = b + m[..., None]
    m_t = jnp.maximum(inter, jnp.max(dmat, axis=-1))
    w_inter = jnp.exp(inter - m_t)
    qk = jnp.einsum('bhtd,bhsd->bhts', q, k) * jnp.exp(dmat - m_t[..., None])
    num = w_inter[..., None] * jnp.einsum('bhtd,bhde->bhte', q, C) + jnp.einsum('bhts,bhse->bhte', qk, v)
    den = w_inter * jnp.einsum('bhtd,bhd->bht', q, n) + jnp.sum(qk, axis=-1)
    h = num / jnp.maximum(jnp.abs(den), jnp.exp(-m_t))[..., None]
    b_last = b[..., -1]
    g_s = b_last[..., None] - b + li
    m_new = jnp.maximum(b_last + m, jnp.max(g_s, axis=-1))
    decay = jnp.exp(b_last + m - m_new)
    w_s = jnp.exp(g_s - m_new[..., None])
    C_new = decay[..., None, None] * C + jnp.einsum('bhs,bhsd,bhse->bhde', w_s, k, v)
    n_new = decay[..., None] * n + jnp.einsum('bhs,bhsd->bhd', w_s, k)
    return (C_new, n_new, m_new), h


def mlstm_scan(q, k, v, li, lf, C0, n0, m0):
    bsz, nh, s, d = q.shape
    chunk = math.gcd(s, A_CHUNK)
    nc = s // chunk

    def split(a):
        return jnp.moveaxis(a.reshape(a.shape[:2] + (nc, chunk) + a.shape[3:]), 2, 0)

    (C, n, m), hs = lax.scan(mlstm_chunk_step, (C0, n0, m0),
                             (split(q), split(k), split(v), split(li), split(lf)))
    h = jnp.moveaxis(hs, 0, 2).reshape(bsz, nh, s, d)
    return h, C, n, m


def mlstm_mixer(x, w_in, b_gate, gn_g, w_out, C0, n0, m0):
    bsz, s, _ = x.shape
    proj = x @ w_in
    q, k, v, o_pre, gates = jnp.split(proj, [D_MODEL, 2 * D_MODEL, 3 * D_MODEL, 4 * D_MODEL], axis=-1)

    def heads(a):
        return a.reshape(bsz, s, A_HEADS, A_HEAD_DIM).transpose(0, 2, 1, 3).astype(F32)

    gates = (gates.astype(F32) + b_gate.astype(F32)).transpose(0, 2, 1)
    log_i = gates[:, :A_HEADS]
    log_f = jax.nn.log_sigmoid(gates[:, A_HEADS:])
    h, C, n, m = mlstm_scan(heads(q), heads(k) * A_HEAD_DIM ** -0.5, heads(v), log_i, log_f,
                            C0.astype(F32), n0.astype(F32), m0.astype(F32))
    mu = jnp.mean(h, axis=-1, keepdims=True)
    var = jnp.mean(jnp.square(h - mu), axis=-1, keepdims=True)
    h = ((h - mu) * lax.rsqrt(var + HEAD_NORM_EPS)).transpose(0, 2, 1, 3).reshape(bsz, s, D_MODEL)
    h = h * gn_g.astype(F32)
    y = (jax.nn.sigmoid(o_pre.astype(F32)) * h).astype(x.dtype) @ w_out
    return y, C, n, m


def log_forget_suffix(lf):
    return lax.cumsum(lf, axis=1, reverse=True) - lf


def fox_prompt(q, k, v, lf):
    bsz, s, nh, d = q.shape
    qb = math.gcd(s, Q_BLOCK)
    nb = s // qb
    suf = log_forget_suffix(lf).transpose(0, 2, 1)
    q_blocks = jnp.moveaxis(q.reshape(bsz, nb, qb, nh, d), 1, 0)
    suf_blocks = jnp.moveaxis(suf.reshape(bsz, nh, nb, qb), 2, 0)
    pos_blocks = jnp.arange(s).reshape(nb, qb)
    pos_k = jnp.arange(s)

    def one_block(args):
        qi, sq, pq = args
        logits = jnp.einsum('bthd,bshd->bhts', qi, k).astype(F32) * B_HEAD_DIM ** -0.5
        logits = logits + suf[:, :, None, :] - sq[..., None]
        logits = jnp.where(pos_k[None, :] <= pq[:, None], logits, -jnp.inf)
        prob = jax.nn.softmax(logits, axis=-1).astype(v.dtype)
        return jnp.einsum('bhts,bshd->bthd', prob, v)

    out = lax.map(one_block, (q_blocks, suf_blocks, pos_blocks))
    return jnp.moveaxis(out, 0, 1).reshape(bsz, s, nh, d)


def fox_sample(q, k, v, lf, cache_k, cache_v, cache_logf, page_table):
    db, t, nh, d = q.shape
    past = page_table.shape[1] * cache_k.shape[1]
    kp = cache_k[page_table].reshape(db, past, nh, d)
    vp = cache_v[page_table].reshape(db, past, nh, d)
    lfp = cache_logf[page_table].reshape(db, past, nh).astype(F32)
    suf = log_forget_suffix(jnp.concatenate([lfp, lf], axis=1)).transpose(0, 2, 1)
    suf_q = suf[:, :, past:]
    scale = B_HEAD_DIM ** -0.5
    logits = jnp.concatenate([jnp.einsum('bthd,bshd->bhts', q, kp).astype(F32),
                              jnp.einsum('bthd,bshd->bhts', q, k).astype(F32)], axis=-1) * scale
    logits = logits + suf[:, :, None, :] - suf_q[..., None]
    pos_k = jnp.arange(past + t)
    pos_q = past + jnp.arange(t)
    logits = jnp.where(pos_k[None, :] <= pos_q[:, None], logits, -jnp.inf)
    prob = jax.nn.softmax(logits, axis=-1).astype(v.dtype)
    return (jnp.einsum('bhts,bshd->bthd', prob[..., :past], vp)
            + jnp.einsum('bhts,bshd->bthd', prob[..., past:], v))


def peer(x, w_q, subkeys, u_tab, v_tab):
    bsz, s, _ = x.shape
    ntok = bsz * s
    tb = math.gcd(ntok, PEER_TOKEN_BLOCK)
    xt = x.reshape(ntok // tb, tb, D_MODEL)
    kk = PEER_TOPK

    def block(xb):
        q = (xb @ w_q).reshape(tb, PEER_HEADS, 2, PEER_KEY_DIM // 2).astype(F32)
        sc = jnp.einsum('thcd,cnd->thcn', q, subkeys.astype(F32))
        sv, si = lax.top_k(sc, kk)
        cand = sv[:, :, 0, :, None] + sv[:, :, 1, None, :]
        cidx = si[:, :, 0, :, None] * PEER_NKEYS + si[:, :, 1, None, :]
        cv, ci = lax.top_k(cand.reshape(tb, PEER_HEADS, kk * kk), kk)
        eidx = jnp.take_along_axis(cidx.reshape(tb, PEER_HEADS, kk * kk), ci, axis=-1)
        g = jax.nn.softmax(cv, axis=-1)
        act = jax.nn.gelu(jnp.einsum('td,thkd->thk', xb, u_tab[eidx]).astype(F32), approximate=False)
        return jnp.einsum('thk,thkd->td', (g * act).astype(x.dtype), v_tab[eidx])

    return lax.map(block, xt).reshape(bsz, s, D_MODEL)


def run_trunk(x, p, C0, n0, m0, fox_fn, a_w_in, a_b_gate, a_gn_g, a_w_out, kv_ln_g, kv_ln_b, kv_w, kv_b_f,
              b_w_q, b_w_o, ln_mix_g, ln_mix_b, ln_ffn_g, ln_ffn_b, peer_w_q, peer_subkeys, peer_u, peer_v,
              ple_w_p, ple_w_g, ple_b_g):
    bsz, s, _ = x.shape
    Cs, ns, ms = [], [], []
    k_sh = v_sh = lf_sh = None
    for i in range(DEPTH):
        if i < N_A_LAYERS:
            y, C, n, m = mlstm_mixer(x, a_w_in[i], a_b_gate[i], a_gn_g[i], a_w_out[i], C0[i], n0[i], m0[i])
            Cs.append(C)
            ns.append(n)
            ms.append(m)
        else:
            if i == N_A_LAYERS:
                st = layer_norm(x, kv_ln_g, kv_ln_b)
                kvf = st @ kv_w
                k_sh = kvf[..., :D_MODEL].reshape(bsz, s, B_HEADS, B_HEAD_DIM)
                v_sh = kvf[..., D_MODEL:2 * D_MODEL].reshape(bsz, s, B_HEADS, B_HEAD_DIM)
                lf_sh = jax.nn.log_sigmoid(kvf[..., 2 * D_MODEL:].astype(F32) + kv_b_f.astype(F32))
            j = i - N_A_LAYERS
            q = (x @ b_w_q[j]).reshape(bsz, s, B_HEADS, B_HEAD_DIM)
            o = fox_fn(q, k_sh, v_sh, lf_sh)
            y = o.reshape(bsz, s, D_MODEL) @ b_w_o[j]
        x = layer_norm(ALPHA * x + y, ln_mix_g[i], ln_mix_b[i])
        x = layer_norm(ALPHA * x + peer(x, peer_w_q[i], peer_subkeys[i], peer_u[i], peer_v[i]),
                       ln_ffn_g[i], ln_ffn_b[i])
        gate = jax.nn.sigmoid((x @ ple_w_g[i] + ple_b_g[i]).astype(F32))
        x = x + (gate * (p[i] @ ple_w_p[i]).astype(F32)).astype(x.dtype)
    return x, k_sh, v_sh, lf_sh, jnp.stack(Cs), jnp.stack(ns), jnp.stack(ms)


def setup_inputs(seed: int = 0) -> dict:
    key = jax.random.key(seed)
    ks = iter(jax.random.split(key, 64))
    n_pages = PAST_LEN // PAGE_SIZE
    n_pool = (DEC_BATCH * n_pages * 5) // 4
    sd = D_MODEL ** -0.5

    def nrm(shape, scale):
        return jax.random.normal(next(ks), shape, F32) * scale

    def unif(shape, lo, hi):
        return jax.random.uniform(next(ks), shape, F32, minval=lo, maxval=hi)

    inp = {}
    inp['x_prompt'] = nrm((BATCH, SEQ, D_MODEL), 1.0)
    inp['x_sample'] = nrm((DEC_BATCH, DEC_SEQ, D_MODEL), 1.0)
    inp['cache_k'] = nrm((n_pool, PAGE_SIZE, B_HEADS, B_HEAD_DIM), 1.0)
    inp['cache_v'] = nrm((n_pool, PAGE_SIZE, B_HEADS, B_HEAD_DIM), BETA)
    inp['cache_logf'] = jax.nn.log_sigmoid(unif((n_pool, PAGE_SIZE, B_HEADS), 1.0, 6.0))
    inp['state_C'] = nrm((N_A_LAYERS, DEC_BATCH, A_HEADS, A_HEAD_DIM, A_HEAD_DIM), 0.1)
    inp['state_n'] = nrm((N_A_LAYERS, DEC_BATCH, A_HEADS, A_HEAD_DIM), 0.1)
    inp['state_m'] = unif((N_A_LAYERS, DEC_BATCH, A_HEADS), -1.0, 1.0)
    inp['page_table'] = jax.random.permutation(next(ks), n_pool)[:DEC_BATCH * n_pages].reshape(
        DEC_BATCH, n_pages).astype(jnp.int32)
    inp['p_prompt'] = nrm((DEPTH, BATCH, SEQ, PLE_DIM), 1.0)
    inp['p_sample'] = nrm((DEPTH, DEC_BATCH, DEC_SEQ, PLE_DIM), 1.0)
    inp['a_w_in'] = jnp.concatenate([
        nrm((N_A_LAYERS, D_MODEL, D_MODEL), sd),
        nrm((N_A_LAYERS, D_MODEL, D_MODEL), sd),
        nrm((N_A_LAYERS, D_MODEL, D_MODEL), sd * BETA),
        nrm((N_A_LAYERS, D_MODEL, D_MODEL), sd),
        nrm((N_A_LAYERS, D_MODEL, 2 * A_HEADS), sd)], axis=-1)
    inp['a_b_gate'] = jnp.concatenate([nrm((N_A_LAYERS, A_HEADS), 0.1),
                                       unif((N_A_LAYERS, A_HEADS), 3.0, 6.0)], axis=-1)
    inp['a_gn_g'] = 1.0 + nrm((N_A_LAYERS, D_MODEL), 0.02)
    inp['a_w_out'] = nrm((N_A_LAYERS, D_MODEL, D_MODEL), sd * BETA)
    inp['kv_ln_g'] = 1.0 + nrm((D_MODEL,), 0.02)
    inp['kv_ln_b'] = nrm((D_MODEL,), 0.02)
    inp['kv_w'] = jnp.concatenate([nrm((D_MODEL, D_MODEL), sd),
                                   nrm((D_MODEL, D_MODEL), sd * BETA),
                                   nrm((D_MODEL, B_HEADS), sd)], axis=-1)
    inp['kv_b_f'] = unif((B_HEADS,), 1.0, 6.0)
    inp['b_w_q'] = nrm((N_B_LAYERS, D_MODEL, D_MODEL), sd)
    inp['b_w_o'] = nrm((N_B_LAYERS, D_MODEL, D_MODEL), sd * BETA)
    inp['ln_mix_g'] = 1.0 + nrm((DEPTH, D_MODEL), 0.02)
    inp['ln_mix_b'] = nrm((DEPTH, D_MODEL), 0.02)
    inp['ln_ffn_g'] = 1.0 + nrm((DEPTH, D_MODEL), 0.02)
    inp['ln_ffn_b'] = nrm((DEPTH, D_MODEL), 0.02)
    inp['peer_w_q'] = nrm((DEPTH, D_MODEL, PEER_HEADS * PEER_KEY_DIM), sd)
    inp['peer_subkeys'] = nrm((DEPTH, 2, PEER_NKEYS, PEER_KEY_DIM // 2), (PEER_KEY_DIM // 2) ** -0.5)
    inp['peer_u'] = nrm((DEPTH, PEER_EXPERTS, D_MODEL), sd)
    inp['peer_v'] = nrm((DEPTH, PEER_EXPERTS, D_MODEL), BETA)
    inp['ple_w_p'] = nrm((DEPTH, PLE_DIM, D_MODEL), PLE_DIM ** -0.5)
    inp['ple_w_g'] = nrm((DEPTH, D_MODEL, D_MODEL), sd)
    inp['ple_b_g'] = nrm((DEPTH, D_MODEL), 0.02)
    return inp


def reference(x_prompt, x_sample, cache_k, cache_v, cache_logf, state_C, state_n, state_m, page_table,
              p_prompt, p_sample, a_w_in, a_b_gate, a_gn_g, a_w_out, kv_ln_g, kv_ln_b, kv_w, kv_b_f,
              b_w_q, b_w_o, ln_mix_g, ln_mix_b, ln_ffn_g, ln_ffn_b, peer_w_q, peer_subkeys, peer_u, peer_v,
              ple_w_p, ple_w_g, ple_b_g):
    bp = x_prompt.shape[0]
    C0p = jnp.zeros((N_A_LAYERS, bp, A_HEADS, A_HEAD_DIM, A_HEAD_DIM), F32)
    n0p = jnp.zeros((N_A_LAYERS, bp, A_HEADS, A_HEAD_DIM), F32)
    m0p = jnp.zeros((N_A_LAYERS, bp, A_HEADS), F32)
    (y_prompt, k_prompt, v_prompt, lf_prompt, C_prompt, n_prompt, m_prompt) = run_trunk(
        x_prompt, p_prompt, C0p, n0p, m0p, fox_prompt,
        a_w_in, a_b_gate, a_gn_g, a_w_out, kv_ln_g, kv_ln_b, kv_w, kv_b_f, b_w_q, b_w_o,
        ln_mix_g, ln_mix_b, ln_ffn_g, ln_ffn_b, peer_w_q, peer_subkeys, peer_u, peer_v,
        ple_w_p, ple_w_g, ple_b_g)
    fox_dec = functools.partial(fox_sample, cache_k=cache_k, cache_v=cache_v,
                                cache_logf=cache_logf, page_table=page_table)
    (y_sample, k_sample, v_sample, lf_sample, C_sample, n_sample, m_sample) = run_trunk(
        x_sample, p_sample, state_C, state_n, state_m, fox_dec,
        a_w_in, a_b_gate, a_gn_g, a_w_out, kv_ln_g, kv_ln_b, kv_w, kv_b_f, b_w_q, b_w_o,
        ln_mix_g, ln_mix_b, ln_ffn_g, ln_ffn_b, peer_w_q, peer_subkeys, peer_u, peer_v,
        ple_w_p, ple_w_g, ple_b_g)
    sdt = state_C.dtype
    ldt = cache_logf.dtype
    return (y_prompt, y_sample,
            k_prompt, v_prompt, lf_prompt.astype(ldt),
            C_prompt.astype(sdt), n_prompt.astype(sdt), m_prompt.astype(sdt),
            k_sample, v_sample, lf_sample.astype(ldt),
            C_sample.astype(sdt), n_sample.astype(sdt), m_sample.astype(sdt))
```

```python
import functools
import math

import jax
import jax.numpy as jnp
from jax import lax
from jax.experimental import pallas as pl
from jax.experimental.pallas import tpu as pltpu

F32 = jnp.float32
BF16 = jnp.bfloat16
I32 = jnp.int32

D = 1024
HEADS = 8
HD = 128
DEPTH = 2
PEER_HEADS = 8
PEER_NKEYS = 128
PEER_TOPK = 16
ALPHA = (2.0 * DEPTH) ** 0.25
LN_EPS = 1e-5
HEAD_NORM_EPS = 1e-6
CHUNK = 128
DEC_PAD = 256

LANES = 128
VMEM_LIMIT = 56 * 1024 * 1024

NEG_INF = float("-inf")


def _cparams(sem, vmem=None):
    return pltpu.CompilerParams(dimension_semantics=sem, vmem_limit_bytes=vmem or VMEM_LIMIT)


def _layer_norm(z, g, b):
    mu = jnp.mean(z, axis=-1, keepdims=True)
    zc = z - mu
    var = jnp.mean(zc * zc, axis=-1, keepdims=True)
    return zc * lax.rsqrt(var + LN_EPS) * g + b


def _log_sigmoid(z):
    return jnp.minimum(z, 0.0) - jnp.log1p(jnp.exp(-jnp.abs(z)))


def _split_bf16(a):
    hi = a.astype(BF16)
    lo = (a - hi.astype(F32)).astype(BF16)
    return hi, lo


def _dot(a, b):
    return jnp.dot(a, b, preferred_element_type=F32)


def _dot_nt(a, b):
    return lax.dot_general(a, b, (((1,), (1,)), ((), ())), preferred_element_type=F32)


def _dot3(x, w):
    xh, xl = _split_bf16(x)
    wh, wl = _split_bf16(w)
    return _dot(xh, wh) + _dot(xl, wh) + _dot(xh, wl)


def _mm_kernel(x_ref, w_ref, o_ref, xb_ref):
    @pl.when(pl.program_id(1) == 0)
    def _():
        xb_ref[...] = x_ref[...].astype(BF16)

    o_ref[...] = _dot(xb_ref[...], w_ref[...])


def matmul(x, w_bf, tm=512, tn=512):
    m, k = x.shape
    n = w_bf.shape[1]
    tm = min(tm, m)
    tn = min(tn, n)
    return pl.pallas_call(
        _mm_kernel,
        grid=(m // tm, n // tn),
        in_specs=[pl.BlockSpec((tm, k), lambda i, j: (i, 0)),
                  pl.BlockSpec((k, tn), lambda i, j: (0, j))],
        out_specs=pl.BlockSpec((tm, tn), lambda i, j: (i, j)),
        out_shape=jax.ShapeDtypeStruct((m, n), F32),
        scratch_shapes=[pltpu.VMEM((tm, k), BF16)],
        compiler_params=_cparams(("parallel", "arbitrary")),
        name="matmul",
    )(x, w_bf)


def _gates_kernel(x_ref, w_ref, b_ref, o_ref, *, ls_start):
    z = _dot3(x_ref[...], w_ref[...]) + b_ref[...]
    col = lax.broadcasted_iota(I32, z.shape, 1)
    o_ref[...] = jnp.where(col >= ls_start, _log_sigmoid(z), z)


def gates_matmul(x, w_pad, b_pad, ls_start, tm=512):
    m, k = x.shape
    tm = min(tm, m)
    return pl.pallas_call(
        functools.partial(_gates_kernel, ls_start=ls_start),
        grid=(m // tm,),
        in_specs=[pl.BlockSpec((tm, k), lambda i: (i, 0)),
                  pl.BlockSpec((k, LANES), lambda i: (0, 0)),
                  pl.BlockSpec((1, LANES), lambda i: (0, 0))],
        out_specs=pl.BlockSpec((tm, LANES), lambda i: (i, 0)),
        out_shape=jax.ShapeDtypeStruct((m, LANES), F32),
        compiler_params=_cparams(("parallel",)),
        name="gates_matmul",
    )(x, w_pad, b_pad)


def _mm_ln_kernel(a_ref, w_ref, r_ref, g_ref, b_ref, o_ref):
    y = _dot(a_ref[...].astype(BF16), w_ref[...])
    o_ref[...] = _layer_norm(ALPHA * r_ref[...] + y, g_ref[...], b_ref[...])


def matmul_ln(a, w_bf, res, g, b, tm=512):
    m, k = a.shape
    n = w_bf.shape[1]
    tm = min(tm, m)
    return pl.pallas_call(
        _mm_ln_kernel,
        grid=(m // tm,),
        in_specs=[pl.BlockSpec((tm, k), lambda i: (i, 0)),
                  pl.BlockSpec((k, n), lambda i: (0, 0)),
                  pl.BlockSpec((tm, n), lambda i: (i, 0)),
                  pl.BlockSpec((1, n), lambda i: (0, 0)),
                  pl.BlockSpec((1, n), lambda i: (0, 0))],
        out_specs=pl.BlockSpec((tm, n), lambda i: (i, 0)),
        out_shape=jax.ShapeDtypeStruct((m, n), F32),
        compiler_params=_cparams(("parallel",)),
        name="matmul_ln",
    )(a, w_bf, res, g.reshape(1, n), b.reshape(1, n))


def _ln_ple_kernel(xm_ref, po_ref, g_ref, b_ref, p_ref, wg_ref, bg_ref, wp_ref, o_ref):
    x2 = _layer_norm(ALPHA * xm_ref[...] + po_ref[...], g_ref[...], b_ref[...])
    gate = jax.nn.sigmoid(_dot(x2.astype(BF16), wg_ref[...]) + bg_ref[...])
    emb = _dot(p_ref[...].astype(BF16), wp_ref[...])
    o_ref[...] = x2 + gate * emb


def ln_ple(xm, po, g, b, p, wg_bf, bg, wp_bf, tm=512):
    m, n = xm.shape
    kp = p.shape[1]
    tm = min(tm, m)
    row = lambda i: (i, 0)
    fixed = lambda i: (0, 0)
    return pl.pallas_call(
        _ln_ple_kernel,
        grid=(m // tm,),
        in_specs=[pl.BlockSpec((tm, n), row), pl.BlockSpec((tm, n), row),
                  pl.BlockSpec((1, n), fixed), pl.BlockSpec((1, n), fixed),
                  pl.BlockSpec((tm, kp), row),
                  pl.BlockSpec((n, n), fixed), pl.BlockSpec((1, n), fixed),
                  pl.BlockSpec((kp, n), fixed)],
        out_specs=pl.BlockSpec((tm, n), row),
        out_shape=jax.ShapeDtypeStruct((m, n), F32),
        compiler_params=_cparams(("parallel",)),
        name="ln_ple",
    )(xm, po, g.reshape(1, n), b.reshape(1, n), p, wg_bf, bg.reshape(1, n), wp_bf)


def _ln_kv_kernel(x_ref, g_ref, b_ref, wk_ref, wv_ref, wf_ref, bf_ref, k_ref, v_ref, lf_ref):
    st = _layer_norm(x_ref[...], g_ref[...], b_ref[...])
    sb = st.astype(BF16)
    k_ref[...] = _dot(sb, wk_ref[...])
    v_ref[...] = _dot(sb, wv_ref[...])
    lf_ref[...] = _log_sigmoid(_dot3(st, wf_ref[...]) + bf_ref[...])


def ln_kv(x, g, b, wk_bf, wv_bf, wf_pad, bf_pad, tm=512):
    m, n = x.shape
    tm = min(tm, m)
    row = lambda i: (i, 0)
    fixed = lambda i: (0, 0)
    return pl.pallas_call(
        _ln_kv_kernel,
        grid=(m // tm,),
        in_specs=[pl.BlockSpec((tm, n), row), pl.BlockSpec((1, n), fixed), pl.BlockSpec((1, n), fixed),
                  pl.BlockSpec((n, n), fixed), pl.BlockSpec((n, n), fixed),
                  pl.BlockSpec((n, LANES), fixed), pl.BlockSpec((1, LANES), fixed)],
        out_specs=[pl.BlockSpec((tm, n), row), pl.BlockSpec((tm, n), row), pl.BlockSpec((tm, LANES), row)],
        out_shape=[jax.ShapeDtypeStruct((m, n), F32), jax.ShapeDtypeStruct((m, n), F32),
                   jax.ShapeDtypeStruct((m, LANES), F32)],
        compiler_params=_cparams(("parallel",)),
        name="ln_kv",
    )(x, g.reshape(1, n), b.reshape(1, n), wk_bf, wv_bf, wf_pad, bf_pad)


def _head_norm_gate(hh, o_pre, gn):
    mu = jnp.mean(hh, axis=-1, keepdims=True)
    hc = hh - mu
    var = jnp.mean(hc * hc, axis=-1, keepdims=True)
    return jax.nn.sigmoid(o_pre) * (hc * lax.rsqrt(var + HEAD_NORM_EPS)) * gn


def _mlstm_prompt_kernel(q_ref, k_ref, v_ref, o_ref, g_ref, gn_ref,
                         y_ref, c_out_ref, n_out_ref, m_out_ref, c_ref, m_ref):
    c_idx = pl.program_id(1)
    L = CHUNK

    @pl.when(c_idx == 0)
    def _():
        c_ref[...] = jnp.zeros_like(c_ref)
        m_ref[...] = jnp.zeros_like(m_ref)

    row = lax.broadcasted_iota(I32, (L, L), 0)
    col = lax.broadcasted_iota(I32, (L, L), 1)
    causal = col <= row
    tril = causal.astype(F32)
    g = g_ref[...]
    cum = jnp.dot(tril, g, preferred_element_type=F32, precision=lax.Precision.HIGHEST)
    g_t = g.T
    cum_t = cum.T
    lane = lax.broadcasted_iota(I32, (L, HD), 1)
    ones_col = (lane == 0).astype(BF16)
    scale = HD ** -0.5

    for h in range(HEADS):
        sl = slice(h * HD, (h + 1) * HD)
        qb = q_ref[:, sl].astype(BF16)
        kb = (k_ref[:, sl] * scale).astype(BF16)
        v = v_ref[:, sl]
        vb = v.astype(BF16)
        li_c = g[:, h:h + 1]
        b_c = cum[:, HEADS + h:HEADS + h + 1]
        li_r = g_t[h:h + 1, :]
        b_r = cum_t[HEADS + h:HEADS + h + 1, :]
        m_prev = m_ref[h:h + 1, 0:1]
        dmat = jnp.where(causal, b_c - b_r + li_r, NEG_INF)
        inter = b_c + m_prev
        m_t = jnp.maximum(inter, jnp.max(dmat, axis=1, keepdims=True))
        w_inter = jnp.exp(inter - m_t)
        qk = _dot_nt(qb, kb) * jnp.exp(dmat - m_t)
        c_aug = c_ref[h]
        a_inter = _dot(qb, c_aug.astype(BF16))
        v_aug = jnp.concatenate([vb, ones_col], axis=1)
        a_intra = _dot(qk.astype(BF16), v_aug)
        num = w_inter * a_inter[:, :HD] + a_intra[:, :HD]
        den = w_inter * a_inter[:, HD:HD + 1] + a_intra[:, HD:HD + 1]
        hh = num / jnp.maximum(jnp.abs(den), jnp.exp(-m_t))
        y_ref[:, sl] = _head_norm_gate(hh, o_ref[:, sl], gn_ref[:, sl])

        b_last = b_c[L - 1:L, :]
        g_c = b_last - b_c + li_c
        m_new = jnp.maximum(b_last + m_prev, jnp.max(g_c, axis=0, keepdims=True))
        decay = jnp.exp(b_last + m_prev - m_new)
        w_c = jnp.exp(g_c - m_new)
        vw = jnp.concatenate([v * w_c, jnp.where(lane == 0, w_c, 0.0)], axis=1).astype(BF16)
        upd = lax.dot_general(kb, vw, (((0,), (0,)), ((), ())), preferred_element_type=F32)
        c_ref[h] = decay * c_aug + upd
        m_ref[h:h + 1, :] = jnp.broadcast_to(m_new, (1, LANES))

    @pl.when(c_idx == pl.num_programs(1) - 1)
    def _():
        for h in range(HEADS):
            c_aug = c_ref[h]
            c_out_ref[0, h] = c_aug[:, :HD]
            n_out_ref[0, h:h + 1, :] = c_aug[:, HD:].T[0:1, :]
        m_out_ref[0] = m_ref[...]


def mlstm_prompt(qkvo, gates, gn_g, bsz, seq):
    nc = seq // CHUNK
    t = bsz * seq

    def col_spec(gidx):
        return pl.BlockSpec((CHUNK, D), lambda b, c: (b * nc + c, gidx))

    return pl.pallas_call(
        _mlstm_prompt_kernel,
        grid=(bsz, nc),
        in_specs=[col_spec(0), col_spec(1), col_spec(2), col_spec(3),
                  pl.BlockSpec((CHUNK, LANES), lambda b, c: (b * nc + c, 0)),
                  pl.BlockSpec((1, D), lambda b, c: (0, 0))],
        out_specs=[pl.BlockSpec((CHUNK, D), lambda b, c: (b * nc + c, 0)),
                   pl.BlockSpec((1, HEADS, HD, HD), lambda b, c: (b, 0, 0, 0)),
                   pl.BlockSpec((1, HEADS, HD), lambda b, c: (b, 0, 0)),
                   pl.BlockSpec((1, HEADS, LANES), lambda b, c: (b, 0, 0))],
        out_shape=[jax.ShapeDtypeStruct((t, D), F32),
                   jax.ShapeDtypeStruct((bsz, HEADS, HD, HD), F32),
                   jax.ShapeDtypeStruct((bsz, HEADS, HD), F32),
                   jax.ShapeDtypeStruct((bsz, HEADS, LANES), F32)],
        scratch_shapes=[pltpu.VMEM((HEADS, HD, 2 * HD), F32), pltpu.VMEM((HEADS, LANES), F32)],
        compiler_params=_cparams(("parallel", "arbitrary")),
        name="mlstm_prompt",
    )(qkvo, qkvo, qkvo, qkvo, gates, gn_g.reshape(1, D))


def _mlstm_step_kernel(q_ref, k_ref, v_ref, o_ref, g_ref, gn_ref, c0_ref, n0_ref, m0_ref,
                       y_ref, c_out_ref, n_out_ref, m_out_ref):
    row = lax.broadcasted_iota(I32, (HD, HD), 0)
    col = lax.broadcasted_iota(I32, (HD, HD), 1)
    eye = row == col
    scale = HD ** -0.5
    g = g_ref[0]
    m0 = m0_ref[0]
    m_new_all = jnp.zeros((1, LANES), F32)
    lane = lax.broadcasted_iota(I32, (1, LANES), 1)

    def to_col(r):
        return jnp.sum(jnp.where(eye, r, 0.0), axis=1, keepdims=True)

    for h in range(HEADS):
        sl = slice(h * HD, (h + 1) * HD)
        q = q_ref[0][:, sl]
        ks = k_ref[0][:, sl] * scale
        v = v_ref[0][:, sl]
        li = g[:, h:h + 1]
        lf = g[:, HEADS + h:HEADS + h + 1]
        m_prev = m0[:, h:h + 1]
        c0 = c0_ref[0, h]
        n0 = n0_ref[0, h:h + 1, :]
        inter = lf + m_prev
        m_t = jnp.maximum(inter, li)
        w_inter = jnp.exp(inter - m_t)
        p = jnp.exp(li - m_t)
        qk = jnp.sum(q * ks, axis=1, keepdims=True) * p
        q_c = jnp.sum(to_col(q) * c0, axis=0, keepdims=True)
        q_n = jnp.sum(q * n0, axis=1, keepdims=True)
        num = w_inter * q_c + qk * v
        den = w_inter * q_n + qk
        hh = num / jnp.maximum(jnp.abs(den), jnp.exp(-m_t))
        y_ref[0, :, sl] = _head_norm_gate(hh, o_ref[0][:, sl], gn_ref[:, sl])
        decay = w_inter
        c_out_ref[0, h] = decay * c0 + to_col(p * ks) * v
        n_out_ref[0, h:h + 1, :] = decay * n0 + p * ks
        m_new_all = jnp.where(lane == h, m_t, m_new_all)
    m_out_ref[0] = m_new_all


def mlstm_step(qkvo3, gates3, gn_g, c0, n0, m0):
    nb = c0.shape[0]

    def col_spec(gidx):
        return pl.BlockSpec((1, 1, D), lambda b: (b, 0, gidx))

    return pl.pallas_call(
        _mlstm_step_kernel,
        grid=(nb,),
        in_specs=[col_spec(0), col_spec(1), col_spec(2), col_spec(3),
                  pl.BlockSpec((1, 1, LANES), lambda b: (b, 0, 0)),
                  pl.BlockSpec((1, D), lambda b: (0, 0)),
                  pl.BlockSpec((1, HEADS, HD, HD), lambda b: (b, 0, 0, 0)),
                  pl.BlockSpec((1, HEADS, HD), lambda b: (b, 0, 0)),
                  pl.BlockSpec((1, 1, HEADS), lambda b: (b, 0, 0))],
        out_specs=[pl.BlockSpec((1, 1, D), lambda b: (b, 0, 0)),
                   pl.BlockSpec((1, HEADS, HD, HD), lambda b: (b, 0, 0, 0)),
                   pl.BlockSpec((1, HEADS, HD), lambda b: (b, 0, 0)),
                   pl.BlockSpec((1, 1, LANES), lambda b: (b, 0, 0))],
        out_shape=[jax.ShapeDtypeStruct((nb, 1, D), F32),
                   jax.ShapeDtypeStruct((nb, HEADS, HD, HD), F32),
                   jax.ShapeDtypeStruct((nb, HEADS, HD), F32),
                   jax.ShapeDtypeStruct((nb, 1, LANES), F32)],
        compiler_params=_cparams(("parallel",)),
        name="mlstm_step",
    )(qkvo3, qkvo3, qkvo3, qkvo3, gates3, gn_g.reshape(1, D), c0, n0, m0.reshape(nb, 1, HEADS))


def _suffix_kernel(lf_ref, o_ref):
    s = lf_ref.shape[2]
    row = lax.broadcasted_iota(I32, (LANES, LANES), 0)
    col = lax.broadcasted_iota(I32, (LANES, LANES), 1)
    upper = (row > col).astype(F32)
    carry = jnp.zeros((HEADS, 1), F32)
    for c in reversed(range(s // LANES)):
        x = lf_ref[0, :, c * LANES:(c + 1) * LANES]
        inner = jnp.dot(x, upper, preferred_element_type=F32, precision=lax.Precision.HIGHEST)
        o_ref[0, :, c * LANES:(c + 1) * LANES] = inner + carry
        carry = carry + jnp.sum(x, axis=1, keepdims=True)


def forget_suffix(lf_t):
    bsz, nh, s = lf_t.shape
    return pl.pallas_call(
        _suffix_kernel,
        grid=(bsz,),
        in_specs=[pl.BlockSpec((1, nh, s), lambda b: (b, 0, 0))],
        out_specs=pl.BlockSpec((1, nh, s), lambda b: (b, 0, 0)),
        out_shape=jax.ShapeDtypeStruct((bsz, nh, s), F32),
        compiler_params=_cparams(("parallel",)),
        name="forget_suffix",
    )(lf_t)


def _fox_prompt_kernel(q_ref, k_ref, v_ref, sq_ref, sk_ref, o_ref, acc_ref, m_ref, l_ref, *, tq):
    i = pl.program_id(1)
    j = pl.program_id(2)
    scale = HD ** -0.5

    @pl.when(j == 0)
    def _():
        acc_ref[...] = jnp.zeros_like(acc_ref)
        m_ref[...] = jnp.full_like(m_ref, NEG_INF)
        l_ref[...] = jnp.zeros_like(l_ref)

    @pl.when(j <= i)
    def _():
        row = lax.broadcasted_iota(I32, (tq, tq), 0)
        col = lax.broadcasted_iota(I32, (tq, tq), 1)
        visible = (col <= row) | (j < i)
        sq = sq_ref[...]
        sk = sk_ref[0]
        for h in range(HEADS):
            sl = slice(h * HD, (h + 1) * HD)
            s = _dot_nt(q_ref[:, sl].astype(BF16), k_ref[:, sl].astype(BF16)) * scale
            s = s + (sk[h:h + 1, :] - sq[:, h:h + 1])
            s = jnp.where(visible, s, NEG_INF)
            m_prev = m_ref[:, h:h + 1]
            m_new = jnp.maximum(m_prev, jnp.max(s, axis=1, keepdims=True))
            alpha = jnp.exp(m_prev - m_new)
            p = jnp.exp(s - m_new)
            l_ref[:, h:h + 1] = alpha * l_ref[:, h:h + 1] + jnp.sum(p, axis=1, keepdims=True)
            acc_ref[:, sl] = alpha * acc_ref[:, sl] + _dot(p.astype(BF16), v_ref[:, sl].astype(BF16))
            m_ref[:, h:h + 1] = m_new

    @pl.when(j == i)
    def _():
        for h in range(HEADS):
            sl = slice(h * HD, (h + 1) * HD)
            o_ref[:, sl] = acc_ref[:, sl] / l_ref[:, h:h + 1]


def fox_prompt(q, k, v, suf_tok, suf_t, bsz, seq, tq=512):
    tq = min(tq, seq)
    nq = seq // tq
    t = bsz * seq
    kv_spec = pl.BlockSpec((tq, D), lambda b, i, j: (b * nq + jnp.minimum(j, i), 0))
    return pl.pallas_call(
        functools.partial(_fox_prompt_kernel, tq=tq),
        grid=(bsz, nq, nq),
        in_specs=[pl.BlockSpec((tq, D), lambda b, i, j: (b * nq + i, 0)),
                  kv_spec, kv_spec,
                  pl.BlockSpec((tq, HEADS), lambda b, i, j: (b * nq + i, 0)),
                  pl.BlockSpec((1, HEADS, tq), lambda b, i, j: (b, 0, jnp.minimum(j, i)))],
        out_specs=pl.BlockSpec((tq, D), lambda b, i, j: (b * nq + i, 0)),
        out_shape=jax.ShapeDtypeStruct((t, D), F32),
        scratch_shapes=[pltpu.VMEM((tq, D), F32), pltpu.VMEM((tq, LANES), F32), pltpu.VMEM((tq, LANES), F32)],
        compiler_params=_cparams(("parallel", "parallel", "arbitrary")),
        name="fox_prompt",
    )(q, k, v, suf_tok, suf_t)


def _fox_decode_kernel(pt_ref, q_ref, kn_ref, vn_ref, lfn_ref, kp_ref, vp_ref, lfp_ref, o_ref,
                       acc_ref, m_ref, l_ref, r_ref):
    j = pl.program_id(1)
    scale = HD ** -0.5
    hrow = lax.broadcasted_iota(I32, (HEADS, D), 0)
    hcol = lax.broadcasted_iota(I32, (HEADS, D), 1) // HD
    blockdiag = hrow == hcol
    qbd = jnp.where(blockdiag, q_ref[0], 0.0)

    @pl.when(j == 0)
    def _():
        s_new = jnp.sum(qbd * kn_ref[0], axis=1, keepdims=True) * scale
        m_ref[...] = jnp.broadcast_to(s_new, m_ref.shape)
        l_ref[...] = jnp.ones_like(l_ref)
        acc_ref[...] = jnp.where(blockdiag, vn_ref[0], 0.0)
        r_ref[...] = lfn_ref[0]

    row = lax.broadcasted_iota(I32, (LANES, LANES), 0)
    col = lax.broadcasted_iota(I32, (LANES, LANES), 1)
    upper = (row > col).astype(F32)
    lf = lfp_ref[0]
    run = r_ref[:, 0:1]
    suf = jnp.dot(lf, upper, preferred_element_type=F32, precision=lax.Precision.HIGHEST) + run
    r_ref[...] = r_ref[...] + jnp.sum(lf, axis=1, keepdims=True)

    s = _dot_nt(qbd.astype(BF16), kp_ref[0].astype(BF16)) * scale + suf
    m_prev = m_ref[:, 0:1]
    m_new = jnp.maximum(m_prev, jnp.max(s, axis=1, keepdims=True))
    alpha = jnp.exp(m_prev - m_new)
    p = jnp.exp(s - m_new)
    l_ref[...] = alpha * l_ref[...] + jnp.sum(p, axis=1, keepdims=True)
    pv = _dot(p.astype(BF16), vp_ref[0].astype(BF16))
    acc_ref[...] = alpha * acc_ref[...] + jnp.where(blockdiag, pv, 0.0)
    m_ref[...] = jnp.broadcast_to(m_new, m_ref.shape)

    @pl.when(j == pl.num_programs(1) - 1)
    def _():
        o_ref[0] = jnp.sum(acc_ref[...] / l_ref[:, 0:1], axis=0, keepdims=True)


def fox_decode(q3, k3, v3, lfn3, cache_k2, cache_v2, cache_lf_t, page_table):
    nb, n_pages = page_table.shape
    page = cache_k2.shape[1]
    tok = lambda b, j, pt: (b, 0, 0)
    pg = lambda b, j, pt: (pt[b, n_pages - 1 - j], 0, 0)
    grid_spec = pltpu.PrefetchScalarGridSpec(
        num_scalar_prefetch=1,
        grid=(nb, n_pages),
        in_specs=[pl.BlockSpec((1, 1, D), tok), pl.BlockSpec((1, 1, D), tok), pl.BlockSpec((1, 1, D), tok),
                  pl.BlockSpec((1, HEADS, LANES), tok),
                  pl.BlockSpec((1, page, D), pg), pl.BlockSpec((1, page, D), pg),
                  pl.BlockSpec((1, HEADS, page), pg)],
        out_specs=pl.BlockSpec((1, 1, D), tok),
        scratch_shapes=[pltpu.VMEM((HEADS, D), F32), pltpu.VMEM((HEADS, LANES), F32),
                        pltpu.VMEM((HEADS, LANES), F32), pltpu.VMEM((HEADS, LANES), F32)],
    )
    return pl.pallas_call(
        _fox_decode_kernel,
        grid_spec=grid_spec,
        out_shape=jax.ShapeDtypeStruct((nb, 1, D), F32),
        compiler_params=_cparams(("parallel", "arbitrary")),
        name="fox_decode",
    )(page_table, q3, k3, v3, lfn3, cache_k2, cache_v2, cache_lf_t)


def _extract_topk(s, tie, k, on_pick):
    big = 3.0e38
    for r in range(k):
        m = jnp.max(s, axis=0, keepdims=True)
        pos = jnp.min(jnp.where(s == m, tie, big), axis=0, keepdims=True)
        onehot = tie == pos
        on_pick(r, m, onehot)
        s = jnp.where(onehot, NEG_INF, s)


def _peer_select_kernel(x_ref, wq_ref, sk_ref, idx_ref, g_ref, sv_ref, si_ref, e_ref, gg_ref):
    tb = x_ref.shape[0]
    kk = PEER_TOPK
    nk = PEER_NKEYS
    q = _dot(x_ref[...].astype(BF16), wq_ref[...]).astype(BF16)
    key_id = lax.broadcasted_iota(I32, (nk, tb), 0).astype(F32)
    sub8 = lax.broadcasted_iota(I32, (8, tb), 0)
    sub16 = lax.broadcasted_iota(I32, (16, tb), 0)

    for h in range(PEER_HEADS):
        for c in range(2):
            qc = q[:, (2 * h + c) * nk:(2 * h + c + 1) * nk]
            s = _dot_nt(sk_ref[c], qc)

            def pick1(r, m, onehot, c=c):
                sv_ref[c, r:r + 1, :] = m
                si_ref[c, r:r + 1, :] = jnp.max(jnp.where(onehot, key_id, -1.0), axis=0, keepdims=True)

            _extract_topk(s, key_id, kk, pick1)

        cand, flat, cid = [], [], []
        for a in range(kk):
            nb = kk // (a + 1)
            rows = 16 if a == 0 else 8
            sub = sub16 if a == 0 else sub8
            va = sv_ref[0, a:a + 1, :]
            ia = si_ref[0, a:a + 1, :]
            piece = va + sv_ref[1, 0:rows, :]
            cand.append(piece if nb >= rows else jnp.where(sub < nb, piece, NEG_INF))
            flat.append((sub + a * kk).astype(F32))
            cid.append(ia * float(nk) + si_ref[1, 0:rows, :])
        cand = jnp.concatenate(cand, axis=0)
        flat = jnp.concatenate(flat, axis=0)
        cid = jnp.concatenate(cid, axis=0)

        def pick2(r, m, onehot, h=h):
            gg_ref[h * kk + r:h * kk + r + 1, :] = m
            e_ref[h * kk + r:h * kk + r + 1, :] = jnp.max(jnp.where(onehot, cid, -1.0), axis=0, keepdims=True)

        _extract_topk(cand, flat, kk, pick2)
        cv = gg_ref[h * kk:(h + 1) * kk, :]
        ex = jnp.exp(cv - cv[0:1, :])
        gg_ref[h * kk:(h + 1) * kk, :] = ex / jnp.sum(ex, axis=0, keepdims=True)

    idx_ref[...] = e_ref[...].T.astype(I32)
    g_ref[...] = gg_ref[...].T


def peer_select(x, wq_bf, sk_bf, tb=256):
    m, n = x.shape
    nq = wq_bf.shape[1]
    np_ = PEER_HEADS * PEER_TOPK
    tb = min(tb, m)
    return pl.pallas_call(
        _peer_select_kernel,
        grid=(m // tb,),
        in_specs=[pl.BlockSpec((tb, n), lambda i: (i, 0)),
                  pl.BlockSpec((n, nq), lambda i: (0, 0)),
                  pl.BlockSpec((2, PEER_NKEYS, PEER_NKEYS), lambda i: (0, 0, 0))],
        out_specs=[pl.BlockSpec((tb, np_), lambda i: (i, 0)), pl.BlockSpec((tb, np_), lambda i: (i, 0))],
        out_shape=[jax.ShapeDtypeStruct((m, np_), I32), jax.ShapeDtypeStruct((m, np_), F32)],
        scratch_shapes=[pltpu.VMEM((2, PEER_TOPK, tb), F32), pltpu.VMEM((2, PEER_TOPK, tb), F32),
                        pltpu.VMEM((np_, tb), F32), pltpu.VMEM((np_, tb), F32)],
        compiler_params=_cparams(("parallel",)),
        name="peer_select",
    )(x, wq_bf, sk_bf)


PICKS = PEER_HEADS * PEER_TOPK
TILE_ROWS = 16
GROWS = PICKS * TILE_ROWS


def _diag_mask():
    sub = lax.broadcasted_iota(I32, (8, GROWS), 0)
    lane = lax.broadcasted_iota(I32, (8, GROWS), 1)
    return (lane % TILE_ROWS) // 2 == sub


def _gather_tiles(idx_smem, tab_ref, g_ref, t):
    for p in range(PICKS):
        e = idx_smem[t, p]
        g_ref[p * TILE_ROWS:(p + 1) * TILE_ROWS, :] = tab_ref[e >> 1]


def _peer_up_kernel(idx_smem, x_ref, idxv_ref, gate_ref, tab_ref, ecomp_ref, w_ref, g_ref, a_ref):
    tbu = x_ref.shape[0]
    mdiag = _diag_mask()

    def body(t, carry):
        _gather_tiles(idx_smem, tab_ref, g_ref, t)
        x8 = x_ref[t].astype(BF16)
        res = _dot_nt(x8, g_ref[...])
        a_ref[pl.ds(t, 1), :] = jnp.sum(jnp.where(mdiag, res, 0.0), axis=0, keepdims=True)
        return carry

    lax.fori_loop(0, tbu, body, 0)
    hi, lo = _split_bf16(a_ref[...])
    both = _dot(hi, ecomp_ref[...]) + _dot(lo, ecomp_ref[...])
    odd = (idxv_ref[...] & 1) == 1
    a = jnp.where(odd, both[:, PICKS:], both[:, :PICKS])
    w_ref[...] = gate_ref[...] * (0.5 * a * (1.0 + lax.erf(a * (2.0 ** -0.5))))


def _peer_down_kernel(idx_smem, w_ref, idxv_ref, tab_ref, eexp_ref, o_ref, g_ref, wx_ref):
    tbu = w_ref.shape[0]
    mdiag = _diag_mask()
    w = w_ref[...]
    odd = (idxv_ref[...] & 1) == 1
    w01 = jnp.concatenate([jnp.where(odd, 0.0, w), jnp.where(odd, w, 0.0)], axis=1).astype(BF16)
    wx_ref[...] = _dot(w01, eexp_ref[...])

    def body(t, carry):
        _gather_tiles(idx_smem, tab_ref, g_ref, t)
        wm = jnp.where(mdiag, wx_ref[pl.ds(t, 1), :], 0.0).astype(BF16)
        o_ref[t] = _dot(wm, g_ref[...])
        return carry

    lax.fori_loop(0, tbu, body, 0)


def _expand_matrix():
    r = lax.broadcasted_iota(I32, (2 * PICKS, GROWS), 0)
    k = lax.broadcasted_iota(I32, (2 * PICKS, GROWS), 1)
    return ((k // TILE_ROWS == r % PICKS) & (k % 2 == r // PICKS)).astype(BF16)


def _table_spec(tab):
    return pl.BlockSpec(tab.shape, lambda i: (0, 0, 0), pipeline_mode=pl.Buffered(1))


def peer_up(idx, x, gate, tab, tbu=64):
    m = x.shape[0]
    tbu = min(tbu, m)
    ecomp = _expand_matrix().T
    return pl.pallas_call(
        _peer_up_kernel,
        grid=(m // tbu,),
        in_specs=[pl.BlockSpec((tbu, PICKS), lambda i: (i, 0), memory_space=pltpu.SMEM),
                  pl.BlockSpec((tbu, 8, LANES), lambda i: (i, 0, 0)),
                  pl.BlockSpec((tbu, PICKS), lambda i: (i, 0)),
                  pl.BlockSpec((tbu, PICKS), lambda i: (i, 0)),
                  _table_spec(tab),
                  pl.BlockSpec((GROWS, 2 * PICKS), lambda i: (0, 0))],
        out_specs=pl.BlockSpec((tbu, PICKS), lambda i: (i, 0)),
        out_shape=jax.ShapeDtypeStruct((m, PICKS), F32),
        scratch_shapes=[pltpu.VMEM((GROWS, LANES), BF16), pltpu.VMEM((tbu, GROWS), F32)],
        compiler_params=_cparams(("arbitrary",)),
        name="peer_up",
    )(idx, x.reshape(m, 8, LANES), idx, gate, tab, ecomp)


def peer_down(idx, w, tab, tbu=64):
    m = w.shape[0]
    tbu = min(tbu, m)
    eexp = _expand_matrix()
    out = pl.pallas_call(
        _peer_down_kernel,
        grid=(m // tbu,),
        in_specs=[pl.BlockSpec((tbu, PICKS), lambda i: (i, 0), memory_space=pltpu.SMEM),
                  pl.BlockSpec((tbu, PICKS), lambda i: (i, 0)),
                  pl.BlockSpec((tbu, PICKS), lambda i: (i, 0)),
                  _table_spec(tab),
                  pl.BlockSpec((2 * PICKS, GROWS), lambda i: (0, 0))],
        out_specs=pl.BlockSpec((tbu, 8, LANES), lambda i: (i, 0, 0)),
        out_shape=jax.ShapeDtypeStruct((m, 8, LANES), F32),
        scratch_shapes=[pltpu.VMEM((GROWS, LANES), BF16), pltpu.VMEM((tbu, GROWS), F32)],
        compiler_params=_cparams(("arbitrary",)),
        name="peer_down",
    )(idx, w, idx, tab, eexp)
    return out.reshape(m, D)


def _pack_table(tab):
    e = tab.shape[0]
    t = tab.astype(BF16).reshape(e // 2, 2, D // LANES, LANES)
    return t.transpose(0, 2, 1, 3).reshape(e // 2, TILE_ROWS, LANES)


def peer(x, wq_bf, sk_bf, u_tab, v_tab):
    idx, gate = peer_select(x, wq_bf, sk_bf)
    w = peer_up(idx, x, gate, u_tab)
    return peer_down(idx, w, v_tab)


def _pad_cols(w, n=LANES):
    return jnp.pad(w, ((0, 0), (0, n - w.shape[1])))


def _pad_rows(a, rows):
    return jnp.pad(a, ((0, rows - a.shape[0]),) + ((0, 0),) * (a.ndim - 1))


def kernel(x_prompt, x_sample, cache_k, cache_v, cache_logf, state_C, state_n, state_m, page_table, p_prompt, p_sample, a_w_in, a_b_gate, a_gn_g, a_w_out, kv_ln_g, kv_ln_b, kv_w, kv_b_f, b_w_q, b_w_o, ln_mix_g, ln_mix_b, ln_ffn_g, ln_ffn_b, peer_w_q, peer_subkeys, peer_u, peer_v, ple_w_p, ple_w_g, ple_b_g):
    bsz, seq, _ = x_prompt.shape
    db = x_sample.shape[0]
    n_pool, page = cache_k.shape[:2]
    sdt = state_C.dtype
    ldt = cache_logf.dtype

    w_in_bf = a_w_in[0][:, :4 * D].astype(BF16)
    w_gate = _pad_cols(a_w_in[0][:, 4 * D:])
    b_gate = _pad_cols(a_b_gate[0].reshape(1, -1))
    w_out_bf = a_w_out[0].astype(BF16)
    wk_bf = kv_w[:, :D].astype(BF16)
    wv_bf = kv_w[:, D:2 * D].astype(BF16)
    wf = _pad_cols(kv_w[:, 2 * D:])
    bf = _pad_cols(kv_b_f.reshape(1, -1))
    bwq_bf = b_w_q[0].astype(BF16)
    bwo_bf = b_w_o[0].astype(BF16)
    pwq_bf = peer_w_q.astype(BF16)
    sk_bf = peer_subkeys.astype(BF16)
    u_tabs = [_pack_table(peer_u[i]) for i in range(DEPTH)]
    v_tabs = [_pack_table(peer_v[i]) for i in range(DEPTH)]
    wg_bf = ple_w_g.astype(BF16)
    wp_bf = ple_w_p.astype(BF16)

    def channel_mix(i, x_mid, p):
        po = peer(x_mid, pwq_bf[i], sk_bf[i], u_tabs[i], v_tabs[i])
        return ln_ple(x_mid, po, ln_ffn_g[i], ln_ffn_b[i], p, wg_bf[i], ple_b_g[i], wp_bf[i])

    def shared_kv(x):
        k, v, lfp = ln_kv(x, kv_ln_g, kv_ln_b, wk_bf, wv_bf, wf, bf)
        return k, v, lfp[:, :HEADS]

    t = bsz * seq
    x = x_prompt.reshape(t, D)
    p = p_prompt.reshape(DEPTH, t, -1)
    qkvo = matmul(x, w_in_bf)
    gates = gates_matmul(x, w_gate, b_gate, HEADS)
    y_pre, c_p, n_p, m_p = mlstm_prompt(qkvo, gates, a_gn_g[0], bsz, seq)
    x = matmul_ln(y_pre, w_out_bf, x, ln_mix_g[0], ln_mix_b[0])
    x = channel_mix(0, x, p[0])

    k_p, v_p, lf_p = shared_kv(x)
    lf_t = lf_p.reshape(bsz, seq, HEADS).transpose(0, 2, 1)
    suf_t = forget_suffix(lf_t)
    suf_tok = suf_t.transpose(0, 2, 1).reshape(t, HEADS)
    q = matmul(x, bwq_bf)
    o = fox_prompt(q, k_p, v_p, suf_tok, suf_t, bsz, seq)
    x = matmul_ln(o, bwo_bf, x, ln_mix_g[1], ln_mix_b[1])
    y_prompt = channel_mix(1, x, p[1]).reshape(bsz, seq, D)

    xs = _pad_rows(x_sample.reshape(db, D), DEC_PAD)
    ps = jnp.pad(p_sample.reshape(DEPTH, db, -1), ((0, 0), (0, DEC_PAD - db), (0, 0)))
    qkvo_s = matmul(xs, w_in_bf)
    gates_s = gates_matmul(xs, w_gate, b_gate, HEADS)
    y_s, c_s, n_s, m_s = mlstm_step(qkvo_s[:db].reshape(db, 1, 4 * D), gates_s[:db].reshape(db, 1, LANES),
                                    a_gn_g[0], state_C[0].astype(F32), state_n[0].astype(F32),
                                    state_m[0].astype(F32))
    xs = matmul_ln(_pad_rows(y_s.reshape(db, D), DEC_PAD), w_out_bf, xs, ln_mix_g[0], ln_mix_b[0])
    xs = channel_mix(0, xs, ps[0])

    k_s, v_s, lf_s = shared_kv(xs)
    q_s = matmul(xs, bwq_bf)
    lfn = jnp.broadcast_to(lf_s[:db].reshape(db, HEADS, 1), (db, HEADS, LANES))
    o_s = fox_decode(q_s[:db].reshape(db, 1, D), k_s[:db].reshape(db, 1, D), v_s[:db].reshape(db, 1, D), lfn,
                     cache_k.reshape(n_pool, page, D), cache_v.reshape(n_pool, page, D),
                     cache_logf.astype(F32).transpose(0, 2, 1), page_table)
    xs = matmul_ln(_pad_rows(o_s.reshape(db, D), DEC_PAD), bwo_bf, xs, ln_mix_g[1], ln_mix_b[1])
    y_sample = channel_mix(1, xs, ps[1])[:db].reshape(db, 1, D)

    return (y_prompt, y_sample,
            k_p.reshape(bsz, seq, HEADS, HD), v_p.reshape(bsz, seq, HEADS, HD),
            lf_p.reshape(bsz, seq, HEADS).astype(ldt),
            c_p[None].astype(sdt), n_p[None].astype(sdt), m_p[:, :, 0][None].astype(sdt),
            k_s[:db].reshape(db, 1, HEADS, HD), v_s[:db].reshape(db, 1, HEADS, HD),
            lf_s[:db].reshape(db, 1, HEADS).astype(ldt),
            c_s[None].astype(sdt), n_s[None].astype(sdt), m_s[:, 0, :HEADS][None].astype(sdt))
```

```python
import functools
import math

import jax
import jax.numpy as jnp
from jax import lax
from jax.experimental import pallas as pl
from jax.experimental.pallas import tpu as pltpu

F32 = jnp.float32
BF16 = jnp.bfloat16
I32 = jnp.int32

D = 1024
HEADS = 8
HD = 128
DEPTH = 2
PEER_HEADS = 8
PEER_NKEYS = 128
PEER_TOPK = 16
ALPHA = (2.0 * DEPTH) ** 0.25
LN_EPS = 1e-5
HEAD_NORM_EPS = 1e-6
CHUNK = 128
DEC_PAD = 256

LANES = 128
VMEM_LIMIT = 56 * 1024 * 1024

NEG_INF = float("-inf")


def _cparams(sem, vmem=None):
    return pltpu.CompilerParams(dimension_semantics=sem, vmem_limit_bytes=vmem or VMEM_LIMIT)


def _layer_norm(z, g, b):
    mu = jnp.mean(z, axis=-1, keepdims=True)
    zc = z - mu
    var = jnp.mean(zc * zc, axis=-1, keepdims=True)
    return zc * lax.rsqrt(var + LN_EPS) * g + b


def _log_sigmoid(z):
    return jnp.minimum(z, 0.0) - jnp.log1p(jnp.exp(-jnp.abs(z)))


def _split_bf16(a):
    hi = a.astype(BF16)
    lo = (a - hi.astype(F32)).astype(BF16)
    return hi, lo


def _dot(a, b):
    return jnp.dot(a, b, preferred_element_type=F32)


def _dot_nt(a, b):
    return lax.dot_general(a, b, (((1,), (1,)), ((), ())), preferred_element_type=F32)


def _dot3(x, w):
    xh, xl = _split_bf16(x)
    wh, wl = _split_bf16(w)
    return _dot(xh, wh) + _dot(xl, wh) + _dot(xh, wl)


def _mm_kernel(x_ref, w_ref, o_ref, xb_ref):
    @pl.when(pl.program_id(1) == 0)
    def _():
        xb_ref[...] = x_ref[...].astype(BF16)

    o_ref[...] = _dot(xb_ref[...], w_ref[...])


def matmul(x, w_bf, tm=512, tn=512):
    m, k = x.shape
    n = w_bf.shape[1]
    tm = min(tm, m)
    tn = min(tn, n)
    return pl.pallas_call(
        _mm_kernel,
        grid=(m // tm, n // tn),
        in_specs=[pl.BlockSpec((tm, k), lambda i, j: (i, 0)),
                  pl.BlockSpec((k, tn), lambda i, j: (0, j))],
        out_specs=pl.BlockSpec((tm, tn), lambda i, j: (i, j)),
        out_shape=jax.ShapeDtypeStruct((m, n), F32),
        scratch_shapes=[pltpu.VMEM((tm, k), BF16)],
        compiler_params=_cparams(("parallel", "arbitrary")),
        name="matmul",
    )(x, w_bf)


def _gates_kernel(x_ref, w_ref, b_ref, o_ref, *, ls_start):
    z = _dot3(x_ref[...], w_ref[...]) + b_ref[...]
    col = lax.broadcasted_iota(I32, z.shape, 1)
    o_ref[...] = jnp.where(col >= ls_start, _log_sigmoid(z), z)


def gates_matmul(x, w_pad, b_pad, ls_start, tm=512):
    m, k = x.shape
    tm = min(tm, m)
    return pl.pallas_call(
        functools.partial(_gates_kernel, ls_start=ls_start),
        grid=(m // tm,),
        in_specs=[pl.BlockSpec((tm, k), lambda i: (i, 0)),
                  pl.BlockSpec((k, LANES), lambda i: (0, 0)),
                  pl.BlockSpec((1, LANES), lambda i: (0, 0))],
        out_specs=pl.BlockSpec((tm, LANES), lambda i: (i, 0)),
        out_shape=jax.ShapeDtypeStruct((m, LANES), F32),
        compiler_params=_cparams(("parallel",)),
        name="gates_matmul",
    )(x, w_pad, b_pad)


def _mm_ln_kernel(a_ref, w_ref, r_ref, g_ref, b_ref, o_ref):
    y = _dot(a_ref[...].astype(BF16), w_ref[...])
    o_ref[...] = _layer_norm(ALPHA * r_ref[...] + y, g_ref[...], b_ref[...])


def matmul_ln(a, w_bf, res, g, b, tm=512):
    m, k = a.shape
    n = w_bf.shape[1]
    tm = min(tm, m)
    return pl.pallas_call(
        _mm_ln_kernel,
        grid=(m // tm,),
        in_specs=[pl.BlockSpec((tm, k), lambda i: (i, 0)),
                  pl.BlockSpec((k, n), lambda i: (0, 0)),
                  pl.BlockSpec((tm, n), lambda i: (i, 0)),
                  pl.BlockSpec((1, n), lambda i: (0, 0)),
                  pl.BlockSpec((1, n), lambda i: (0, 0))],
        out_specs=pl.BlockSpec((tm, n), lambda i: (i, 0)),
        out_shape=jax.ShapeDtypeStruct((m, n), F32),
        compiler_params=_cparams(("parallel",)),
        name="matmul_ln",
    )(a, w_bf, res, g.reshape(1, n), b.reshape(1, n))


def _ln_ple_kernel(xm_ref, po_ref, g_ref, b_ref, p_ref, wg_ref, bg_ref, wp_ref, o_ref):
    x2 = _layer_norm(ALPHA * xm_ref[...] + po_ref[...], g_ref[...], b_ref[...])
    gate = jax.nn.sigmoid(_dot(x2.astype(BF16), wg_ref[...]) + bg_ref[...])
    emb = _dot(p_ref[...].astype(BF16), wp_ref[...])
    o_ref[...] = x2 + gate * emb


def ln_ple(xm, po, g, b, p, wg_bf, bg, wp_bf, tm=512):
    m, n = xm.shape
    kp = p.shape[1]
    tm = min(tm, m)
    row = lambda i: (i, 0)
    fixed = lambda i: (0, 0)
    return pl.pallas_call(
        _ln_ple_kernel,
        grid=(m // tm,),
        in_specs=[pl.BlockSpec((tm, n), row), pl.BlockSpec((tm, n), row),
                  pl.BlockSpec((1, n), fixed), pl.BlockSpec((1, n), fixed),
                  pl.BlockSpec((tm, kp), row),
                  pl.BlockSpec((n, n), fixed), pl.BlockSpec((1, n), fixed),
                  pl.BlockSpec((kp, n), fixed)],
        out_specs=pl.BlockSpec((tm, n), row),
        out_shape=jax.ShapeDtypeStruct((m, n), F32),
        compiler_params=_cparams(("parallel",)),
        name="ln_ple",
    )(xm, po, g.reshape(1, n), b.reshape(1, n), p, wg_bf, bg.reshape(1, n), wp_bf)


def _ln_kv_kernel(x_ref, g_ref, b_ref, wk_ref, wv_ref, wf_ref, bf_ref, k_ref, v_ref, lf_ref):
    st = _layer_norm(x_ref[...], g_ref[...], b_ref[...])
    sb = st.astype(BF16)
    k_ref[...] = _dot(sb, wk_ref[...])
    v_ref[...] = _dot(sb, wv_ref[...])
    lf_ref[...] = _log_sigmoid(_dot3(st, wf_ref[...]) + bf_ref[...])


def ln_kv(x, g, b, wk_bf, wv_bf, wf_pad, bf_pad, tm=512):
    m, n = x.shape
    tm = min(tm, m)
    row = lambda i: (i, 0)
    fixed = lambda i: (0, 0)
    return pl.pallas_call(
        _ln_kv_kernel,
        grid=(m // tm,),
        in_specs=[pl.BlockSpec((tm, n), row), pl.BlockSpec((1, n), fixed), pl.BlockSpec((1, n), fixed),
                  pl.BlockSpec((n, n), fixed), pl.BlockSpec((n, n), fixed),
                  pl.BlockSpec((n, LANES), fixed), pl.BlockSpec((1, LANES), fixed)],
        out_specs=[pl.BlockSpec((tm, n), row), pl.BlockSpec((tm, n), row), pl.BlockSpec((tm, LANES), row)],
        out_shape=[jax.ShapeDtypeStruct((m, n), F32), jax.ShapeDtypeStruct((m, n), F32),
                   jax.ShapeDtypeStruct((m, LANES), F32)],
        compiler_params=_cparams(("parallel",)),
        name="ln_kv",
    )(x, g.reshape(1, n), b.reshape(1, n), wk_bf, wv_bf, wf_pad, bf_pad)


def _head_norm_gate(hh, o_pre, gn):
    mu = jnp.mean(hh, axis=-1, keepdims=True)
    hc = hh - mu
    var = jnp.mean(hc * hc, axis=-1, keepdims=True)
    return jax.nn.sigmoid(o_pre) * (hc * lax.rsqrt(var + HEAD_NORM_EPS)) * gn


def _mlstm_prompt_kernel(q_ref, k_ref, v_ref, o_ref, g_ref, gn_ref,
                         y_ref, c_out_ref, n_out_ref, m_out_ref, c_ref, m_ref):
    c_idx = pl.program_id(1)
    L = CHUNK

    @pl.when(c_idx == 0)
    def _():
        c_ref[...] = jnp.zeros_like(c_ref)
        m_ref[...] = jnp.zeros_like(m_ref)

    row = lax.broadcasted_iota(I32, (L, L), 0)
    col = lax.broadcasted_iota(I32, (L, L), 1)
    causal = col <= row
    tril = causal.astype(F32)
    g = g_ref[...]
    cum = jnp.dot(tril, g, preferred_element_type=F32, precision=lax.Precision.HIGHEST)
    g_t = g.T
    cum_t = cum.T
    lane = lax.broadcasted_iota(I32, (L, HD), 1)
    ones_col = (lane == 0).astype(BF16)
    scale = HD ** -0.5

    for h in range(HEADS):
        sl = slice(h * HD, (h + 1) * HD)
        qb = q_ref[:, sl].astype(BF16)
        kb = (k_ref[:, sl] * scale).astype(BF16)
        v = v_ref[:, sl]
        vb = v.astype(BF16)
        li_c = g[:, h:h + 1]
        b_c = cum[:, HEADS + h:HEADS + h + 1]
        li_r = g_t[h:h + 1, :]
        b_r = cum_t[HEADS + h:HEADS + h + 1, :]
        m_prev = m_ref[h:h + 1, 0:1]
        dmat = jnp.where(causal, b_c - b_r + li_r, NEG_INF)
        inter = b_c + m_prev
        m_t = jnp.maximum(inter, jnp.max(dmat, axis=1, keepdims=True))
        w_inter = jnp.exp(inter - m_t)
        qk = _dot_nt(qb, kb) * jnp.exp(dmat - m_t)
        c_aug = c_ref[h]
        a_inter = _dot(qb, c_aug.astype(BF16))
        v_aug = jnp.concatenate([vb, ones_col], axis=1)
        a_intra = _dot(qk.astype(BF16), v_aug)
        num = w_inter * a_inter[:, :HD] + a_intra[:, :HD]
        den = w_inter * a_inter[:, HD:HD + 1] + a_intra[:, HD:HD + 1]
        hh = num / jnp.maximum(jnp.abs(den), jnp.exp(-m_t))
        y_ref[:, sl] = _head_norm_gate(hh, o_ref[:, sl], gn_ref[:, sl])

        b_last = b_c[L - 1:L, :]
        g_c = b_last - b_c + li_c
        m_new = jnp.maximum(b_last + m_prev, jnp.max(g_c, axis=0, keepdims=True))
        decay = jnp.exp(b_last + m_prev - m_new)
        w_c = jnp.exp(g_c - m_new)
        vw = jnp.concatenate([v * w_c, jnp.where(lane == 0, w_c, 0.0)], axis=1).astype(BF16)
        upd = lax.dot_general(kb, vw, (((0,), (0,)), ((), ())), preferred_element_type=F32)
        c_ref[h] = decay * c_aug + upd
        m_ref[h:h + 1, :] = jnp.broadcast_to(m_new, (1, LANES))

    @pl.when(c_idx == pl.num_programs(1) - 1)
    def _():
        for h in range(HEADS):
            c_aug = c_ref[h]
            c_out_ref[0, h] = c_aug[:, :HD]
            n_out_ref[0, h:h + 1, :] = c_aug[:, HD:].T[0:1, :]
        m_out_ref[0] = m_ref[...]


def mlstm_prompt(qkvo, gates, gn_g, bsz, seq):
    nc = seq // CHUNK
    t = bsz * seq

    def col_spec(gidx):
        return pl.BlockSpec((CHUNK, D), lambda b, c: (b * nc + c, gidx))

    return pl.pallas_call(
        _mlstm_prompt_kernel,
        grid=(bsz, nc),
        in_specs=[col_spec(0), col_spec(1), col_spec(2), col_spec(3),
                  pl.BlockSpec((CHUNK, LANES), lambda b, c: (b * nc + c, 0)),
                  pl.BlockSpec((1, D), lambda b, c: (0, 0))],
        out_specs=[pl.BlockSpec((CHUNK, D), lambda b, c: (b * nc + c, 0)),
                   pl.BlockSpec((1, HEADS, HD, HD), lambda b, c: (b, 0, 0, 0)),
                   pl.BlockSpec((1, HEADS, HD), lambda b, c: (b, 0, 0)),
                   pl.BlockSpec((1, HEADS, LANES), lambda b, c: (b, 0, 0))],
        out_shape=[jax.ShapeDtypeStruct((t, D), F32),
                   jax.ShapeDtypeStruct((bsz, HEADS, HD, HD), F32),
                   jax.ShapeDtypeStruct((bsz, HEADS, HD), F32),
                   jax.ShapeDtypeStruct((bsz, HEADS, LANES), F32)],
        scratch_shapes=[pltpu.VMEM((HEADS, HD, 2 * HD), F32), pltpu.VMEM((HEADS, LANES), F32)],
        compiler_params=_cparams(("parallel", "arbitrary")),
        name="mlstm_prompt",
    )(qkvo, qkvo, qkvo, qkvo, gates, gn_g.reshape(1, D))


def _mlstm_step_kernel(q_ref, k_ref, v_ref, o_ref, g_ref, gn_ref, c0_ref, n0_ref, m0_ref,
                       y_ref, c_out_ref, n_out_ref, m_out_ref):
    row = lax.broadcasted_iota(I32, (HD, HD), 0)
    col = lax.broadcasted_iota(I32, (HD, HD), 1)
    eye = row == col
    scale = HD ** -0.5
    g = g_ref[0]
    m0 = m0_ref[0]
    m_new_all = jnp.zeros((1, LANES), F32)
    lane = lax.broadcasted_iota(I32, (1, LANES), 1)

    def to_col(r):
        return jnp.sum(jnp.where(eye, r, 0.0), axis=1, keepdims=True)

    for h in range(HEADS):
        sl = slice(h * HD, (h + 1) * HD)
        q = q_ref[0][:, sl]
        ks = k_ref[0][:, sl] * scale
        v = v_ref[0][:, sl]
        li = g[:, h:h + 1]
        lf = g[:, HEADS + h:HEADS + h + 1]
        m_prev = m0[:, h:h + 1]
        c0 = c0_ref[0, h]
        n0 = n0_ref[0, h:h + 1, :]
        inter = lf + m_prev
        m_t = jnp.maximum(inter, li)
        w_inter = jnp.exp(inter - m_t)
        p = jnp.exp(li - m_t)
        qk = jnp.sum(q * ks, axis=1, keepdims=True) * p
        q_c = jnp.sum(to_col(q) * c0, axis=0, keepdims=True)
        q_n = jnp.sum(q * n0, axis=1, keepdims=True)
        num = w_inter * q_c + qk * v
        den = w_inter * q_n + qk
        hh = num / jnp.maximum(jnp.abs(den), jnp.exp(-m_t))
        y_ref[0, :, sl] = _head_norm_gate(hh, o_ref[0][:, sl], gn_ref[:, sl])
        decay = w_inter
        c_out_ref[0, h] = decay * c0 + to_col(p * ks) * v
        n_out_ref[0, h:h + 1, :] = decay * n0 + p * ks
        m_new_all = jnp.where(lane == h, m_t, m_new_all)
    m_out_ref[0] = m_new_all


def mlstm_step(qkvo3, gates3, gn_g, c0, n0, m0):
    nb = c0.shape[0]

    def col_spec(gidx):
        return pl.BlockSpec((1, 1, D), lambda b: (b, 0, gidx))

    return pl.pallas_call(
        _mlstm_step_kernel,
        grid=(nb,),
        in_specs=[col_spec(0), col_spec(1), col_spec(2), col_spec(3),
                  pl.BlockSpec((1, 1, LANES), lambda b: (b, 0, 0)),
                  pl.BlockSpec((1, D), lambda b: (0, 0)),
                  pl.BlockSpec((1, HEADS, HD, HD), lambda b: (b, 0, 0, 0)),
                  pl.BlockSpec((1, HEADS, HD), lambda b: (b, 0, 0)),
                  pl.BlockSpec((1, 1, HEADS), lambda b: (b, 0, 0))],
        out_specs=[pl.BlockSpec((1, 1, D), lambda b: (b, 0, 0)),
                   pl.BlockSpec((1, HEADS, HD, HD), lambda b: (b, 0, 0, 0)),
                   pl.BlockSpec((1, HEADS, HD), lambda b: (b, 0, 0)),
                   pl.BlockSpec((1, 1, LANES), lambda b: (b, 0, 0))],
        out_shape=[jax.ShapeDtypeStruct((nb, 1, D), F32),
                   jax.ShapeDtypeStruct((nb, HEADS, HD, HD), F32),
                   jax.ShapeDtypeStruct((nb, HEADS, HD), F32),
                   jax.ShapeDtypeStruct((nb, 1, LANES), F32)],
        compiler_params=_cparams(("parallel",)),
        name="mlstm_step",
    )(qkvo3, qkvo3, qkvo3, qkvo3, gates3, gn_g.reshape(1, D), c0, n0, m0.reshape(nb, 1, HEADS))


def _suffix_kernel(lf_ref, o_ref):
    s = lf_ref.shape[2]
    row = lax.broadcasted_iota(I32, (LANES, LANES), 0)
    col = lax.broadcasted_iota(I32, (LANES, LANES), 1)
    upper = (row > col).astype(F32)
    carry = jnp.zeros((HEADS, 1), F32)
    for c in reversed(range(s // LANES)):
        x = lf_ref[0, :, c * LANES:(c + 1) * LANES]
        inner = jnp.dot(x, upper, preferred_element_type=F32, precision=lax.Precision.HIGHEST)
        o_ref[0, :, c * LANES:(c + 1) * LANES] = inner + carry
        carry = carry + jnp.sum(x, axis=1, keepdims=True)


def forget_suffix(lf_t):
    bsz, nh, s = lf_t.shape
    return pl.pallas_call(
        _suffix_kernel,
        grid=(bsz,),
        in_specs=[pl.BlockSpec((1, nh, s), lambda b: (b, 0, 0))],
        out_specs=pl.BlockSpec((1, nh, s), lambda b: (b, 0, 0)),
        out_shape=jax.ShapeDtypeStruct((bsz, nh, s), F32),
        compiler_params=_cparams(("parallel",)),
        name="forget_suffix",
    )(lf_t)


def _fox_prompt_kernel(q_ref, k_ref, v_ref, sq_ref, sk_ref, o_ref, acc_ref, m_ref, l_ref, *, tq):
    i = pl.program_id(1)
    j = pl.program_id(2)
    scale = HD ** -0.5

    @pl.when(j == 0)
    def _():
        acc_ref[...] = jnp.zeros_like(acc_ref)
        m_ref[...] = jnp.full_like(m_ref, NEG_INF)
        l_ref[...] = jnp.zeros_like(l_ref)

    @pl.when(j <= i)
    def _():
        row = lax.broadcasted_iota(I32, (tq, tq), 0)
        col = lax.broadcasted_iota(I32, (tq, tq), 1)
        visible = (col <= row) | (j < i)
        sq = sq_ref[...]
        sk = sk_ref[0]
        for h in range(HEADS):
            sl = slice(h * HD, (h + 1) * HD)
            sqb = jnp.broadcast_to(sq[:, h:h + 1], (tq, LANES))
            s = _dot_nt(q_ref[:, sl].astype(BF16), k_ref[:, sl].astype(BF16)) * scale + sk[h:h + 1, :]
            s = jnp.where(visible, s, NEG_INF)
            m_prev = m_ref[h]
            m_new = jnp.maximum(m_prev, jnp.max(s, axis=1, keepdims=True) - sqb)
            alpha = jnp.exp(m_prev - m_new)
            p = jnp.exp(s - pltpu.repeat(m_new + sqb, tq // LANES, axis=1))
            l_ref[h] = alpha * l_ref[h] + jnp.sum(p, axis=1, keepdims=True)
            acc_ref[:, sl] = alpha * acc_ref[:, sl] + _dot(p.astype(BF16), v_ref[:, sl].astype(BF16))
            m_ref[h] = m_new

    @pl.when(j == i)
    def _():
        for h in range(HEADS):
            sl = slice(h * HD, (h + 1) * HD)
            o_ref[:, sl] = acc_ref[:, sl] / l_ref[h]


def fox_prompt(q, k, v, suf_tok, suf_t, bsz, seq, tq=512):
    tq = min(tq, seq)
    nq = seq // tq
    t = bsz * seq
    kv_spec = pl.BlockSpec((tq, D), lambda b, i, j: (b * nq + jnp.minimum(j, i), 0))
    return pl.pallas_call(
        functools.partial(_fox_prompt_kernel, tq=tq),
        grid=(bsz, nq, nq),
        in_specs=[pl.BlockSpec((tq, D), lambda b, i, j: (b * nq + i, 0)),
                  kv_spec, kv_spec,
                  pl.BlockSpec((tq, HEADS), lambda b, i, j: (b * nq + i, 0)),
                  pl.BlockSpec((1, HEADS, tq), lambda b, i, j: (b, 0, jnp.minimum(j, i)))],
        out_specs=pl.BlockSpec((tq, D), lambda b, i, j: (b * nq + i, 0)),
        out_shape=jax.ShapeDtypeStruct((t, D), F32),
        scratch_shapes=[pltpu.VMEM((tq, D), F32), pltpu.VMEM((HEADS, tq, LANES), F32),
                        pltpu.VMEM((HEADS, tq, LANES), F32)],
        compiler_params=_cparams(("parallel", "parallel", "arbitrary")),
        name="fox_prompt",
    )(q, k, v, suf_tok, suf_t)


def _page_suffix_kernel(lf_ref, suf_ref, tot_ref):
    row = lax.broadcasted_iota(I32, (LANES, LANES), 0)
    col = lax.broadcasted_iota(I32, (LANES, LANES), 1)
    upper = (row > col).astype(F32)
    x = lf_ref[...]
    suf_ref[...] = jnp.dot(x, upper, preferred_element_type=F32, precision=lax.Precision.HIGHEST)
    tot_ref[...] = jnp.broadcast_to(jnp.sum(x, axis=1, keepdims=True), x.shape)


def page_suffix(lf_rows, tr=2048):
    r, page = lf_rows.shape
    tr = math.gcd(r, tr)
    spec = pl.BlockSpec((tr, page), lambda i: (i, 0))
    return pl.pallas_call(
        _page_suffix_kernel,
        grid=(r // tr,),
        in_specs=[spec],
        out_specs=[spec, spec],
        out_shape=[jax.ShapeDtypeStruct((r, page), F32), jax.ShapeDtypeStruct((r, page), F32)],
        compiler_params=_cparams(("parallel",)),
        name="page_suffix",
    )(lf_rows)


def _fox_decode_kernel(pt_ref, q_ref, kn_ref, vn_ref, lfn_ref, *refs, pg):
    page_refs = refs[:4 * pg]
    o_ref, acc_ref, m_ref, l_ref, r_ref = refs[4 * pg:]
    j = pl.program_id(1)
    scale = HD ** -0.5
    hrow = lax.broadcasted_iota(I32, (HEADS, D), 0)
    hcol = lax.broadcasted_iota(I32, (HEADS, D), 1) // HD
    blockdiag = hrow == hcol
    qbd = jnp.where(blockdiag, q_ref[0], 0.0)

    @pl.when(j == 0)
    def _():
        s_new = jnp.sum(qbd * kn_ref[0], axis=1, keepdims=True) * scale
        m_ref[...] = jnp.broadcast_to(s_new, m_ref.shape)
        l_ref[...] = jnp.ones_like(l_ref)
        acc_ref[...] = jnp.where(blockdiag, vn_ref[0], 0.0)
        r_ref[...] = lfn_ref[0]

    run = r_ref[...]
    sufs, ks, vs = [], [], []
    for i in range(pg):
        k_ref, v_ref, suf_ref, tot_ref = page_refs[4 * i:4 * i + 4]
        sufs.append(suf_ref[0] + run)
        run = run + tot_ref[0]
        ks.append(k_ref[0].astype(BF16))
        vs.append(v_ref[0].astype(BF16))
    r_ref[...] = run
    s = _dot_nt(qbd.astype(BF16), jnp.concatenate(ks, axis=0)) * scale + jnp.concatenate(sufs, axis=1)
    m_prev = m_ref[:, 0:1]
    m_new = jnp.maximum(m_prev, jnp.max(s, axis=1, keepdims=True))
    alpha = jnp.exp(m_prev - m_new)
    p = jnp.exp(s - m_new)
    l_ref[...] = alpha * l_ref[...] + jnp.sum(p, axis=1, keepdims=True)
    pv = _dot(p.astype(BF16), jnp.concatenate(vs, axis=0))
    acc_ref[...] = alpha * acc_ref[...] + jnp.where(blockdiag, pv, 0.0)
    m_ref[...] = jnp.broadcast_to(m_new, m_ref.shape)

    @pl.when(j == pl.num_programs(1) - 1)
    def _():
        o_ref[0] = jnp.sum(acc_ref[...] / l_ref[:, 0:1], axis=0, keepdims=True)


def fox_decode(q3, k3, v3, lfn3, cache_k2, cache_v2, cache_lf_t, page_table, pages_per_step=4):
    nb, n_pages = page_table.shape
    n_pool, page = cache_k2.shape[:2]
    pg = math.gcd(n_pages, pages_per_step)
    suf_in, tot = page_suffix(cache_lf_t.reshape(n_pool * HEADS, page))
    suf_in = suf_in.reshape(n_pool, HEADS, page)
    tot = tot.reshape(n_pool, HEADS, page)
    tok = lambda b, j, pt: (b, 0, 0)
    in_specs = [pl.BlockSpec((1, 1, D), tok), pl.BlockSpec((1, 1, D), tok), pl.BlockSpec((1, 1, D), tok),
                pl.BlockSpec((1, HEADS, LANES), tok)]
    operands = []
    for i in range(pg):
        sel = lambda b, j, pt, i=i: (pt[b, n_pages - 1 - (j * pg + i)], 0, 0)
        in_specs += [pl.BlockSpec((1, page, D), sel), pl.BlockSpec((1, page, D), sel),
                     pl.BlockSpec((1, HEADS, page), sel), pl.BlockSpec((1, HEADS, page), sel)]
        operands += [cache_k2, cache_v2, suf_in, tot]
    grid_spec = pltpu.PrefetchScalarGridSpec(
        num_scalar_prefetch=1,
        grid=(nb, n_pages // pg),
        in_specs=in_specs,
        out_specs=pl.BlockSpec((1, 1, D), tok),
        scratch_shapes=[pltpu.VMEM((HEADS, D), F32), pltpu.VMEM((HEADS, LANES), F32),
                        pltpu.VMEM((HEADS, LANES), F32), pltpu.VMEM((HEADS, LANES), F32)],
    )
    return pl.pallas_call(
        functools.partial(_fox_decode_kernel, pg=pg),
        grid_spec=grid_spec,
        out_shape=jax.ShapeDtypeStruct((nb, 1, D), F32),
        compiler_params=_cparams(("parallel", "arbitrary")),
        name="fox_decode",
    )(page_table, q3, k3, v3, lfn3, *operands)


PICKS = PEER_HEADS * PEER_TOPK
EXPERT_BITS = 14
PAIR_ROWS = 8
TILE_ROWS = 2 * PAIR_ROWS
GROWS = PICKS * TILE_ROWS


def _take_topk(s, tie, k, put):
    big = 3.0e38
    for r in range(k):
        m = jnp.max(s, axis=0, keepdims=True)
        pos = jnp.min(jnp.where(s == m, tie, big), axis=0, keepdims=True)
        put(r, m, pos)
        s = jnp.where(tie == pos, NEG_INF, s)


def _peer_select_kernel(x_ref, wq_ref, sk_ref, idx_ref, off_ref, g_ref, sc_ref, sv_ref, si_ref, e_ref, gg_ref):
    tb = x_ref.shape[0]
    kk = PEER_TOPK
    nk = PEER_NKEYS
    lw = LANES
    q = _dot(x_ref[...].astype(BF16), wq_ref[...]).astype(BF16)
    key_id = lax.broadcasted_iota(I32, (nk, lw), 0).astype(F32)
    sub8 = lax.broadcasted_iota(I32, (8, lw), 0)
    sub16 = lax.broadcasted_iota(I32, (16, lw), 0)

    for h in range(PEER_HEADS):
        for c in range(2):
            qc = q[:, (2 * h + c) * nk:(2 * h + c + 1) * nk]
            sc_ref[c] = _dot_nt(sk_ref[c], qc)
        for part in range(tb // lw):
            ls = slice(part * lw, (part + 1) * lw)
            for c in range(2):
                def put1(r, m, pos, c=c):
                    sv_ref[c, r:r + 1, ls] = m
                    si_ref[c, r:r + 1, ls] = pos

                _take_topk(sc_ref[c, :, ls], key_id, kk, put1)

            cand, ckey = [], []
            for a in range(kk):
                nb = kk // (a + 1)
                rows = 16 if a == 0 else 8
                sub = sub16 if a == 0 else sub8
                piece = sv_ref[0, a:a + 1, ls] + sv_ref[1, 0:rows, ls]
                cand.append(piece if nb >= rows else jnp.where(sub < nb, piece, NEG_INF))
                flat = (sub + a * kk).astype(F32)
                ckey.append(flat * float(2 ** EXPERT_BITS) + (si_ref[0, a:a + 1, ls] * float(nk) + si_ref[1, 0:rows, ls]))
            cand = jnp.concatenate(cand, axis=0)
            ckey = jnp.concatenate(ckey, axis=0)

            def put2(r, m, pos, h=h):
                gg_ref[h * kk + r:h * kk + r + 1, ls] = m
                e_ref[h * kk + r:h * kk + r + 1, ls] = pos

            _take_topk(cand, ckey, kk, put2)
            cv = gg_ref[h * kk:(h + 1) * kk, ls]
            ex = jnp.exp(cv - cv[0:1, :])
            gg_ref[h * kk:(h + 1) * kk, ls] = ex / jnp.sum(ex, axis=0, keepdims=True)

    expert = e_ref[...].T.astype(I32) & (2 ** EXPERT_BITS - 1)
    idx_ref[...] = expert
    off_ref[...] = (expert >> 1) * PAIR_ROWS
    g_ref[...] = gg_ref[...].T


def peer_select(x, wq_bf, sk_bf, tb=256):
    m, n = x.shape
    nq = wq_bf.shape[1]
    tb = min(tb, m)
    out_spec = pl.BlockSpec((tb, PICKS), lambda i: (i, 0))
    return pl.pallas_call(
        _peer_select_kernel,
        grid=(m // tb,),
        in_specs=[pl.BlockSpec((tb, n), lambda i: (i, 0)),
                  pl.BlockSpec((n, nq), lambda i: (0, 0)),
                  pl.BlockSpec((2, PEER_NKEYS, PEER_NKEYS), lambda i: (0, 0, 0))],
        out_specs=[out_spec, out_spec, out_spec],
        out_shape=[jax.ShapeDtypeStruct((m, PICKS), I32), jax.ShapeDtypeStruct((m, PICKS), I32),
                   jax.ShapeDtypeStruct((m, PICKS), F32)],
        scratch_shapes=[pltpu.VMEM((2, PEER_NKEYS, tb), F32),
                        pltpu.VMEM((2, PEER_TOPK, tb), F32), pltpu.VMEM((2, PEER_TOPK, tb), F32),
                        pltpu.VMEM((PICKS, tb), F32), pltpu.VMEM((PICKS, tb), F32)],
        compiler_params=_cparams(("parallel",)),
        name="peer_select",
    )(x, wq_bf, sk_bf)


def _diag_mask():
    sub = lax.broadcasted_iota(I32, (8, GROWS), 0)
    lane = lax.broadcasted_iota(I32, (8, GROWS), 1)
    return (lane % TILE_ROWS) // 2 == sub


def _gather_pairs(off_smem, tab_ref, t):
    tiles = []
    for p in range(PICKS):
        off = pl.multiple_of(off_smem[t, p], PAIR_ROWS)
        tiles.append(pltpu.bitcast(tab_ref[pl.ds(off, PAIR_ROWS), :], BF16))
    return jnp.concatenate(tiles, axis=0)


TOKEN_UNROLL = 8


def _peer_up_kernel(off_smem, x_ref, idxv_ref, gate_ref, tab_ref, ecomp_ref, w_ref, a_ref):
    tbu = x_ref.shape[0]
    mdiag = _diag_mask()

    def body(t, carry):
        g = _gather_pairs(off_smem, tab_ref, t)
        x8 = x_ref[t].astype(BF16)
        res = _dot_nt(x8, g)
        a_ref[pl.ds(t, 1), :] = jnp.sum(jnp.where(mdiag, res, 0.0), axis=0, keepdims=True)
        return carry

    lax.fori_loop(0, tbu, body, 0, unroll=TOKEN_UNROLL)
    hi, lo = _split_bf16(a_ref[...])
    both = _dot(hi, ecomp_ref[...]) + _dot(lo, ecomp_ref[...])
    odd = (idxv_ref[...] & 1) == 1
    a = jnp.where(odd, both[:, PICKS:], both[:, :PICKS])
    w_ref[...] = gate_ref[...] * (0.5 * a * (1.0 + lax.erf(a * (2.0 ** -0.5))))


def _peer_down_kernel(off_smem, w_ref, idxv_ref, tab_ref, eexp_ref, o_ref, wx_ref):
    tbu = w_ref.shape[0]
    mdiag = _diag_mask()
    w = w_ref[...]
    odd = (idxv_ref[...] & 1) == 1
    w01 = jnp.concatenate([jnp.where(odd, 0.0, w), jnp.where(odd, w, 0.0)], axis=1).astype(BF16)
    wx_ref[...] = _dot(w01, eexp_ref[...])

    def body(t, carry):
        g = _gather_pairs(off_smem, tab_ref, t)
        wm = jnp.where(mdiag, wx_ref[pl.ds(t, 1), :], 0.0).astype(BF16)
        o_ref[t] = _dot(wm, g)
        return carry

    lax.fori_loop(0, tbu, body, 0, unroll=TOKEN_UNROLL)


def _expand_matrix():
    r = lax.broadcasted_iota(I32, (2 * PICKS, GROWS), 0)
    k = lax.broadcasted_iota(I32, (2 * PICKS, GROWS), 1)
    return ((k // TILE_ROWS == r % PICKS) & (k % 2 == r // PICKS)).astype(BF16)


def _table_spec(tab):
    return pl.BlockSpec(tab.shape, lambda i: (0, 0), pipeline_mode=pl.Buffered(1))


def peer_up(off, idx, x, gate, tab, tbu=64):
    m = x.shape[0]
    tbu = min(tbu, m)
    ecomp = _expand_matrix().T
    return pl.pallas_call(
        _peer_up_kernel,
        grid=(m // tbu,),
        in_specs=[pl.BlockSpec((tbu, PICKS), lambda i: (i, 0), memory_space=pltpu.SMEM),
                  pl.BlockSpec((tbu, 8, LANES), lambda i: (i, 0, 0)),
                  pl.BlockSpec((tbu, PICKS), lambda i: (i, 0)),
                  pl.BlockSpec((tbu, PICKS), lambda i: (i, 0)),
                  _table_spec(tab),
                  pl.BlockSpec((GROWS, 2 * PICKS), lambda i: (0, 0))],
        out_specs=pl.BlockSpec((tbu, PICKS), lambda i: (i, 0)),
        out_shape=jax.ShapeDtypeStruct((m, PICKS), F32),
        scratch_shapes=[pltpu.VMEM((tbu, GROWS), F32)],
        compiler_params=_cparams(("arbitrary",)),
        name="peer_up",
    )(off, x.reshape(m, 8, LANES), idx, gate, tab, ecomp)


def peer_down(off, idx, w, tab, tbu=64):
    m = w.shape[0]
    tbu = min(tbu, m)
    eexp = _expand_matrix()
    out = pl.pallas_call(
        _peer_down_kernel,
        grid=(m // tbu,),
        in_specs=[pl.BlockSpec((tbu, PICKS), lambda i: (i, 0), memory_space=pltpu.SMEM),
                  pl.BlockSpec((tbu, PICKS), lambda i: (i, 0)),
                  pl.BlockSpec((tbu, PICKS), lambda i: (i, 0)),
                  _table_spec(tab),
                  pl.BlockSpec((2 * PICKS, GROWS), lambda i: (0, 0))],
        out_specs=pl.BlockSpec((tbu, 8, LANES), lambda i: (i, 0, 0)),
        out_shape=jax.ShapeDtypeStruct((m, 8, LANES), F32),
        scratch_shapes=[pltpu.VMEM((tbu, GROWS), F32)],
        compiler_params=_cparams(("arbitrary",)),
        name="peer_down",
    )(off, w, idx, tab, eexp)
    return out.reshape(m, D)


def _pack_table(tab):
    e = tab.shape[0]
    bits = lax.bitcast_convert_type(tab.astype(BF16), jnp.uint16).astype(jnp.uint32).reshape(e // 2, 2, D)
    words = bits[:, 0, :] | (bits[:, 1, :] << 16)
    return words.reshape(e // 2 * PAIR_ROWS, LANES)


def peer(x, wq_bf, sk_bf, u_tab, v_tab):
    idx, off, gate = peer_select(x, wq_bf, sk_bf)
    w = peer_up(off, idx, x, gate, u_tab)
    return peer_down(off, idx, w, v_tab)


def _pad_cols(w, n=LANES):
    return jnp.pad(w, ((0, 0), (0, n - w.shape[1])))


def _pad_rows(a, rows):
    return jnp.pad(a, ((0, rows - a.shape[0]),) + ((0, 0),) * (a.ndim - 1))


def kernel(x_prompt, x_sample, cache_k, cache_v, cache_logf, state_C, state_n, state_m, page_table, p_prompt, p_sample, a_w_in, a_b_gate, a_gn_g, a_w_out, kv_ln_g, kv_ln_b, kv_w, kv_b_f, b_w_q, b_w_o, ln_mix_g, ln_mix_b, ln_ffn_g, ln_ffn_b, peer_w_q, peer_subkeys, peer_u, peer_v, ple_w_p, ple_w_g, ple_b_g):
    bsz, seq, _ = x_prompt.shape
    db = x_sample.shape[0]
    n_pool, page = cache_k.shape[:2]
    sdt = state_C.dtype
    ldt = cache_logf.dtype

    w_in_bf = a_w_in[0][:, :4 * D].astype(BF16)
    w_gate = _pad_cols(a_w_in[0][:, 4 * D:])
    b_gate = _pad_cols(a_b_gate[0].reshape(1, -1))
    w_out_bf = a_w_out[0].astype(BF16)
    wk_bf = kv_w[:, :D].astype(BF16)
    wv_bf = kv_w[:, D:2 * D].astype(BF16)
    wf = _pad_cols(kv_w[:, 2 * D:])
    bf = _pad_cols(kv_b_f.reshape(1, -1))
    bwq_bf = b_w_q[0].astype(BF16)
    bwo_bf = b_w_o[0].astype(BF16)
    pwq_bf = peer_w_q.astype(BF16)
    sk_bf = peer_subkeys.astype(BF16)
    u_tabs = [_pack_table(peer_u[i]) for i in range(DEPTH)]
    v_tabs = [_pack_table(peer_v[i]) for i in range(DEPTH)]
    wg_bf = ple_w_g.astype(BF16)
    wp_bf = ple_w_p.astype(BF16)

    def channel_mix(i, x_mid, p):
        po = peer(x_mid, pwq_bf[i], sk_bf[i], u_tabs[i], v_tabs[i])
        return ln_ple(x_mid, po, ln_ffn_g[i], ln_ffn_b[i], p, wg_bf[i], ple_b_g[i], wp_bf[i])

    def shared_kv(x):
        k, v, lfp = ln_kv(x, kv_ln_g, kv_ln_b, wk_bf, wv_bf, wf, bf)
        return k, v, lfp[:, :HEADS]

    t = bsz * seq
    x = x_prompt.reshape(t, D)
    p = p_prompt.reshape(DEPTH, t, -1)
    qkvo = matmul(x, w_in_bf)
    gates = gates_matmul(x, w_gate, b_gate, HEADS)
    y_pre, c_p, n_p, m_p = mlstm_prompt(qkvo, gates, a_gn_g[0], bsz, seq)
    x = matmul_ln(y_pre, w_out_bf, x, ln_mix_g[0], ln_mix_b[0])
    x = channel_mix(0, x, p[0])

    k_p, v_p, lf_p = shared_kv(x)
    lf_t = lf_p.reshape(bsz, seq, HEADS).transpose(0, 2, 1)
    suf_t = forget_suffix(lf_t)
    suf_tok = suf_t.transpose(0, 2, 1).reshape(t, HEADS)
    q = matmul(x, bwq_bf)
    o = fox_prompt(q, k_p, v_p, suf_tok, suf_t, bsz, seq)
    x = matmul_ln(o, bwo_bf, x, ln_mix_g[1], ln_mix_b[1])
    y_prompt = channel_mix(1, x, p[1]).reshape(bsz, seq, D)

    xs = _pad_rows(x_sample.reshape(db, D), DEC_PAD)
    ps = jnp.pad(p_sample.reshape(DEPTH, db, -1), ((0, 0), (0, DEC_PAD - db), (0, 0)))
    qkvo_s = matmul(xs, w_in_bf)
    gates_s = gates_matmul(xs, w_gate, b_gate, HEADS)
    y_s, c_s, n_s, m_s = mlstm_step(qkvo_s[:db].reshape(db, 1, 4 * D), gates_s[:db].reshape(db, 1, LANES),
                                    a_gn_g[0], state_C[0].astype(F32), state_n[0].astype(F32),
                                    state_m[0].astype(F32))
    xs = matmul_ln(_pad_rows(y_s.reshape(db, D), DEC_PAD), w_out_bf, xs, ln_mix_g[0], ln_mix_b[0])
    xs = channel_mix(0, xs, ps[0])

    k_s, v_s, lf_s = shared_kv(xs)
    q_s = matmul(xs, bwq_bf)
    lfn = jnp.broadcast_to(lf_s[:db].reshape(db, HEADS, 1), (db, HEADS, LANES))
    o_s = fox_decode(q_s[:db].reshape(db, 1, D), k_s[:db].reshape(db, 1, D), v_s[:db].reshape(db, 1, D), lfn,
                     cache_k.reshape(n_pool, page, D), cache_v.reshape(n_pool, page, D),
                     cache_logf.astype(F32).transpose(0, 2, 1), page_table)
    xs = matmul_ln(_pad_rows(o_s.reshape(db, D), DEC_PAD), bwo_bf, xs, ln_mix_g[1], ln_mix_b[1])
    y_sample = channel_mix(1, xs, ps[1])[:db].reshape(db, 1, D)

    return (y_prompt, y_sample,
            k_p.reshape(bsz, seq, HEADS, HD), v_p.reshape(bsz, seq, HEADS, HD),
            lf_p.reshape(bsz, seq, HEADS).astype(ldt),
            c_p[None].astype(sdt), n_p[None].astype(sdt), m_p[:, :, 0][None].astype(sdt),
            k_s[:db].reshape(db, 1, HEADS, HD), v_s[:db].reshape(db, 1, HEADS, HD),
            lf_s[:db].reshape(db, 1, HEADS).astype(ldt),
            c_s[None].astype(sdt), n_s[None].astype(sdt), m_s[:, 0, :HEADS][None].astype(sdt))
```

```python
import functools
import math

import jax
import jax.numpy as jnp
from jax import lax
from jax.experimental import pallas as pl
from jax.experimental.pallas import tpu as pltpu

F32 = jnp.float32
BF16 = jnp.bfloat16
I32 = jnp.int32

D = 1024
HEADS = 8
HD = 128
DEPTH = 2
PEER_HEADS = 8
PEER_NKEYS = 128
PEER_TOPK = 16
ALPHA = (2.0 * DEPTH) ** 0.25
LN_EPS = 1e-5
HEAD_NORM_EPS = 1e-6
CHUNK = 128
DEC_PAD = 128

LANES = 128
VMEM_LIMIT = 56 * 1024 * 1024

NEG_INF = float("-inf")


def _cparams(sem, vmem=None):
    return pltpu.CompilerParams(dimension_semantics=sem, vmem_limit_bytes=vmem or VMEM_LIMIT)


def _layer_norm(z, g, b):
    mu = jnp.mean(z, axis=-1, keepdims=True)
    zc = z - mu
    var = jnp.mean(zc * zc, axis=-1, keepdims=True)
    return zc * lax.rsqrt(var + LN_EPS) * g + b


def _log_sigmoid(z):
    return jnp.minimum(z, 0.0) - jnp.log1p(jnp.exp(-jnp.abs(z)))


def _split_bf16(a):
    hi = a.astype(BF16)
    lo = (a - hi.astype(F32)).astype(BF16)
    return hi, lo


def _dot(a, b):
    return jnp.dot(a, b, preferred_element_type=F32)


def _dot_nt(a, b):
    return lax.dot_general(a, b, (((1,), (1,)), ((), ())), preferred_element_type=F32)


def _dot3(x, w):
    xh, xl = _split_bf16(x)
    wh, wl = _split_bf16(w)
    return _dot(xh, wh) + _dot(xl, wh) + _dot(xh, wl)


def _mm_kernel(x_ref, w_ref, o_ref, xb_ref):
    @pl.when(pl.program_id(1) == 0)
    def _():
        xb_ref[...] = x_ref[...].astype(BF16)

    o_ref[...] = _dot(xb_ref[...], w_ref[...])


def matmul(x, w_bf, tm=1024, tn=1024):
    m, k = x.shape
    n = w_bf.shape[1]
    tm = min(tm, m)
    tn = min(tn, n)
    return pl.pallas_call(
        _mm_kernel,
        grid=(m // tm, n // tn),
        in_specs=[pl.BlockSpec((tm, k), lambda i, j: (i, 0)),
                  pl.BlockSpec((k, tn), lambda i, j: (0, j))],
        out_specs=pl.BlockSpec((tm, tn), lambda i, j: (i, j)),
        out_shape=jax.ShapeDtypeStruct((m, n), F32),
        scratch_shapes=[pltpu.VMEM((tm, k), BF16)],
        compiler_params=_cparams(("parallel", "arbitrary")),
        name="matmul",
    )(x, w_bf)


def _gates_kernel(x_ref, w_ref, b_ref, o_ref, *, ls_start):
    z = _dot3(x_ref[...], w_ref[...]) + b_ref[...]
    col = lax.broadcasted_iota(I32, z.shape, 1)
    o_ref[...] = jnp.where(col >= ls_start, _log_sigmoid(z), z)


def gates_matmul(x, w_pad, b_pad, ls_start, tm=512):
    m, k = x.shape
    tm = min(tm, m)
    return pl.pallas_call(
        functools.partial(_gates_kernel, ls_start=ls_start),
        grid=(m // tm,),
        in_specs=[pl.BlockSpec((tm, k), lambda i: (i, 0)),
                  pl.BlockSpec((k, LANES), lambda i: (0, 0)),
                  pl.BlockSpec((1, LANES), lambda i: (0, 0))],
        out_specs=pl.BlockSpec((tm, LANES), lambda i: (i, 0)),
        out_shape=jax.ShapeDtypeStruct((m, LANES), F32),
        compiler_params=_cparams(("parallel",)),
        name="gates_matmul",
    )(x, w_pad, b_pad)


def _mm_ln_kernel(a_ref, w_ref, r_ref, g_ref, b_ref, o_ref):
    y = _dot(a_ref[...].astype(BF16), w_ref[...])
    o_ref[...] = _layer_norm(ALPHA * r_ref[...] + y, g_ref[...], b_ref[...])


def matmul_ln(a, w_bf, res, g, b, tm=512):
    m, k = a.shape
    n = w_bf.shape[1]
    tm = min(tm, m)
    return pl.pallas_call(
        _mm_ln_kernel,
        grid=(m // tm,),
        in_specs=[pl.BlockSpec((tm, k), lambda i: (i, 0)),
                  pl.BlockSpec((k, n), lambda i: (0, 0)),
                  pl.BlockSpec((tm, n), lambda i: (i, 0)),
                  pl.BlockSpec((1, n), lambda i: (0, 0)),
                  pl.BlockSpec((1, n), lambda i: (0, 0))],
        out_specs=pl.BlockSpec((tm, n), lambda i: (i, 0)),
        out_shape=jax.ShapeDtypeStruct((m, n), F32),
        compiler_params=_cparams(("parallel",)),
        name="matmul_ln",
    )(a, w_bf, res, g.reshape(1, n), b.reshape(1, n))


def _ln_ple_kernel(xm_ref, po_ref, g_ref, b_ref, p_ref, wg_ref, bg_ref, wp_ref, o_ref):
    x2 = _layer_norm(ALPHA * xm_ref[...] + po_ref[...], g_ref[...], b_ref[...])
    gate = jax.nn.sigmoid(_dot(x2.astype(BF16), wg_ref[...]) + bg_ref[...])
    emb = _dot(p_ref[...].astype(BF16), wp_ref[...])
    o_ref[...] = x2 + gate * emb


def ln_ple(xm, po, g, b, p, wg_bf, bg, wp_bf, tm=512):
    m, n = xm.shape
    kp = p.shape[1]
    tm = min(tm, m)
    row = lambda i: (i, 0)
    fixed = lambda i: (0, 0)
    return pl.pallas_call(
        _ln_ple_kernel,
        grid=(m // tm,),
        in_specs=[pl.BlockSpec((tm, n), row), pl.BlockSpec((tm, n), row),
                  pl.BlockSpec((1, n), fixed), pl.BlockSpec((1, n), fixed),
                  pl.BlockSpec((tm, kp), row),
                  pl.BlockSpec((n, n), fixed), pl.BlockSpec((1, n), fixed),
                  pl.BlockSpec((kp, n), fixed)],
        out_specs=pl.BlockSpec((tm, n), row),
        out_shape=jax.ShapeDtypeStruct((m, n), F32),
        compiler_params=_cparams(("parallel",)),
        name="ln_ple",
    )(xm, po, g.reshape(1, n), b.reshape(1, n), p, wg_bf, bg.reshape(1, n), wp_bf)


def _ln_kv_kernel(x_ref, g_ref, b_ref, wk_ref, wv_ref, wf_ref, bf_ref, k_ref, v_ref, lf_ref):
    st = _layer_norm(x_ref[...], g_ref[...], b_ref[...])
    sb = st.astype(BF16)
    k_ref[...] = _dot(sb, wk_ref[...])
    v_ref[...] = _dot(sb, wv_ref[...])
    lf_ref[...] = _log_sigmoid(_dot3(st, wf_ref[...]) + bf_ref[...])


def ln_kv(x, g, b, wk_bf, wv_bf, wf_pad, bf_pad, tm=512):
    m, n = x.shape
    tm = min(tm, m)
    row = lambda i: (i, 0)
    fixed = lambda i: (0, 0)
    return pl.pallas_call(
        _ln_kv_kernel,
        grid=(m // tm,),
        in_specs=[pl.BlockSpec((tm, n), row), pl.BlockSpec((1, n), fixed), pl.BlockSpec((1, n), fixed),
                  pl.BlockSpec((n, n), fixed), pl.BlockSpec((n, n), fixed),
                  pl.BlockSpec((n, LANES), fixed), pl.BlockSpec((1, LANES), fixed)],
        out_specs=[pl.BlockSpec((tm, n), row), pl.BlockSpec((tm, n), row), pl.BlockSpec((tm, LANES), row)],
        out_shape=[jax.ShapeDtypeStruct((m, n), F32), jax.ShapeDtypeStruct((m, n), F32),
                   jax.ShapeDtypeStruct((m, LANES), F32)],
        compiler_params=_cparams(("parallel",)),
        name="ln_kv",
    )(x, g.reshape(1, n), b.reshape(1, n), wk_bf, wv_bf, wf_pad, bf_pad)


def _head_norm_gate(hh, o_pre, gn):
    mu = jnp.mean(hh, axis=-1, keepdims=True)
    hc = hh - mu
    var = jnp.mean(hc * hc, axis=-1, keepdims=True)
    return jax.nn.sigmoid(o_pre) * (hc * lax.rsqrt(var + HEAD_NORM_EPS)) * gn


def _mlstm_prompt_kernel(q_ref, k_ref, v_ref, o_ref, g_ref, gn_ref,
                         y_ref, c_out_ref, n_out_ref, m_out_ref, c_ref, m_ref):
    c_idx = pl.program_id(1)
    L = CHUNK

    @pl.when(c_idx == 0)
    def _():
        c_ref[...] = jnp.zeros_like(c_ref)
        m_ref[...] = jnp.zeros_like(m_ref)

    row = lax.broadcasted_iota(I32, (L, L), 0)
    col = lax.broadcasted_iota(I32, (L, L), 1)
    causal = col <= row
    tril = causal.astype(F32)
    g = g_ref[...]
    cum = jnp.dot(tril, g, preferred_element_type=F32, precision=lax.Precision.HIGHEST)
    g_t = g.T
    cum_t = cum.T
    lane = lax.broadcasted_iota(I32, (L, HD), 1)
    ones_col = (lane == 0).astype(BF16)
    scale = HD ** -0.5

    for h in range(HEADS):
        sl = slice(h * HD, (h + 1) * HD)
        qb = q_ref[:, sl].astype(BF16)
        kb = (k_ref[:, sl] * scale).astype(BF16)
        v = v_ref[:, sl]
        vb = v.astype(BF16)
        li_c = g[:, h:h + 1]
        b_c = cum[:, HEADS + h:HEADS + h + 1]
        li_r = g_t[h:h + 1, :]
        b_r = cum_t[HEADS + h:HEADS + h + 1, :]
        m_prev = m_ref[h:h + 1, 0:1]
        dmat = jnp.where(causal, b_c - b_r + li_r, NEG_INF)
        inter = b_c + m_prev
        m_t = jnp.maximum(inter, jnp.max(dmat, axis=1, keepdims=True))
        w_inter = jnp.exp(inter - m_t)
        qk = _dot_nt(qb, kb) * jnp.exp(dmat - m_t)
        c_aug = c_ref[h]
        a_inter = _dot(qb, c_aug.astype(BF16))
        v_aug = jnp.concatenate([vb, ones_col], axis=1)
        a_intra = _dot(qk.astype(BF16), v_aug)
        num = w_inter * a_inter[:, :HD] + a_intra[:, :HD]
        den = w_inter * a_inter[:, HD:HD + 1] + a_intra[:, HD:HD + 1]
        hh = num / jnp.maximum(jnp.abs(den), jnp.exp(-m_t))
        y_ref[:, sl] = _head_norm_gate(hh, o_ref[:, sl], gn_ref[:, sl])

        b_last = b_c[L - 1:L, :]
        g_c = b_last - b_c + li_c
        m_new = jnp.maximum(b_last + m_prev, jnp.max(g_c, axis=0, keepdims=True))
        decay = jnp.exp(b_last + m_prev - m_new)
        w_c = jnp.exp(g_c - m_new)
        vw = jnp.concatenate([v * w_c, jnp.where(lane == 0, w_c, 0.0)], axis=1).astype(BF16)
        upd = lax.dot_general(kb, vw, (((0,), (0,)), ((), ())), preferred_element_type=F32)
        c_ref[h] = decay * c_aug + upd
        m_ref[h:h + 1, :] = jnp.broadcast_to(m_new, (1, LANES))

    @pl.when(c_idx == pl.num_programs(1) - 1)
    def _():
        for h in range(HEADS):
            c_aug = c_ref[h]
            c_out_ref[0, h] = c_aug[:, :HD]
            n_out_ref[0, h:h + 1, :] = c_aug[:, HD:].T[0:1, :]
        m_out_ref[0] = m_ref[...]


def mlstm_prompt(qkvo, gates, gn_g, bsz, seq):
    nc = seq // CHUNK
    t = bsz * seq

    def col_spec(gidx):
        return pl.BlockSpec((CHUNK, D), lambda b, c: (b * nc + c, gidx))

    return pl.pallas_call(
        _mlstm_prompt_kernel,
        grid=(bsz, nc),
        in_specs=[col_spec(0), col_spec(1), col_spec(2), col_spec(3),
                  pl.BlockSpec((CHUNK, LANES), lambda b, c: (b * nc + c, 0)),
                  pl.BlockSpec((1, D), lambda b, c: (0, 0))],
        out_specs=[pl.BlockSpec((CHUNK, D), lambda b, c: (b * nc + c, 0)),
                   pl.BlockSpec((1, HEADS, HD, HD), lambda b, c: (b, 0, 0, 0)),
                   pl.BlockSpec((1, HEADS, HD), lambda b, c: (b, 0, 0)),
                   pl.BlockSpec((1, HEADS, LANES), lambda b, c: (b, 0, 0))],
        out_shape=[jax.ShapeDtypeStruct((t, D), F32),
                   jax.ShapeDtypeStruct((bsz, HEADS, HD, HD), F32),
                   jax.ShapeDtypeStruct((bsz, HEADS, HD), F32),
                   jax.ShapeDtypeStruct((bsz, HEADS, LANES), F32)],
        scratch_shapes=[pltpu.VMEM((HEADS, HD, 2 * HD), F32), pltpu.VMEM((HEADS, LANES), F32)],
        compiler_params=_cparams(("parallel", "arbitrary")),
        name="mlstm_prompt",
    )(qkvo, qkvo, qkvo, qkvo, gates, gn_g.reshape(1, D))


def _mlstm_step_kernel(q_ref, k_ref, v_ref, o_ref, g_ref, gn_ref, c0_ref, n0_ref, m0_ref,
                       y_ref, c_out_ref, n_out_ref, m_out_ref):
    row = lax.broadcasted_iota(I32, (HD, HD), 0)
    col = lax.broadcasted_iota(I32, (HD, HD), 1)
    eye = row == col
    scale = HD ** -0.5
    g = g_ref[0]
    m0 = m0_ref[0]
    m_new_all = jnp.zeros((1, LANES), F32)
    lane = lax.broadcasted_iota(I32, (1, LANES), 1)

    def to_col(r):
        return jnp.sum(jnp.where(eye, r, 0.0), axis=1, keepdims=True)

    for h in range(HEADS):
        sl = slice(h * HD, (h + 1) * HD)
        q = q_ref[0][:, sl]
        ks = k_ref[0][:, sl] * scale
        v = v_ref[0][:, sl]
        li = g[:, h:h + 1]
        lf = g[:, HEADS + h:HEADS + h + 1]
        m_prev = m0[:, h:h + 1]
        c0 = c0_ref[0, h]
        n0 = n0_ref[0, h:h + 1, :]
        inter = lf + m_prev
        m_t = jnp.maximum(inter, li)
        w_inter = jnp.exp(inter - m_t)
        p = jnp.exp(li - m_t)
        qk = jnp.sum(q * ks, axis=1, keepdims=True) * p
        q_c = jnp.sum(to_col(q) * c0, axis=0, keepdims=True)
        q_n = jnp.sum(q * n0, axis=1, keepdims=True)
        num = w_inter * q_c + qk * v
        den = w_inter * q_n + qk
        hh = num / jnp.maximum(jnp.abs(den), jnp.exp(-m_t))
        y_ref[0, :, sl] = _head_norm_gate(hh, o_ref[0][:, sl], gn_ref[:, sl])
        decay = w_inter
        c_out_ref[0, h] = decay * c0 + to_col(p * ks) * v
        n_out_ref[0, h:h + 1, :] = decay * n0 + p * ks
        m_new_all = jnp.where(lane == h, m_t, m_new_all)
    m_out_ref[0] = m_new_all


def mlstm_step(qkvo3, gates3, gn_g, c0, n0, m0):
    nb = c0.shape[0]

    def col_spec(gidx):
        return pl.BlockSpec((1, 1, D), lambda b: (b, 0, gidx))

    return pl.pallas_call(
        _mlstm_step_kernel,
        grid=(nb,),
        in_specs=[col_spec(0), col_spec(1), col_spec(2), col_spec(3),
                  pl.BlockSpec((1, 1, LANES), lambda b: (b, 0, 0)),
                  pl.BlockSpec((1, D), lambda b: (0, 0)),
                  pl.BlockSpec((1, HEADS, HD, HD), lambda b: (b, 0, 0, 0)),
                  pl.BlockSpec((1, HEADS, HD), lambda b: (b, 0, 0)),
                  pl.BlockSpec((1, 1, HEADS), lambda b: (b, 0, 0))],
        out_specs=[pl.BlockSpec((1, 1, D), lambda b: (b, 0, 0)),
                   pl.BlockSpec((1, HEADS, HD, HD), lambda b: (b, 0, 0, 0)),
                   pl.BlockSpec((1, HEADS, HD), lambda b: (b, 0, 0)),
                   pl.BlockSpec((1, 1, LANES), lambda b: (b, 0, 0))],
        out_shape=[jax.ShapeDtypeStruct((nb, 1, D), F32),
                   jax.ShapeDtypeStruct((nb, HEADS, HD, HD), F32),
                   jax.ShapeDtypeStruct((nb, HEADS, HD), F32),
                   jax.ShapeDtypeStruct((nb, 1, LANES), F32)],
        compiler_params=_cparams(("parallel",)),
        name="mlstm_step",
    )(qkvo3, qkvo3, qkvo3, qkvo3, gates3, gn_g.reshape(1, D), c0, n0, m0.reshape(nb, 1, HEADS))


def _suffix_kernel(lf_ref, o_ref):
    s = lf_ref.shape[2]
    row = lax.broadcasted_iota(I32, (LANES, LANES), 0)
    col = lax.broadcasted_iota(I32, (LANES, LANES), 1)
    upper = (row > col).astype(F32)
    carry = jnp.zeros((HEADS, 1), F32)
    for c in reversed(range(s // LANES)):
        x = lf_ref[0, :, c * LANES:(c + 1) * LANES]
        inner = jnp.dot(x, upper, preferred_element_type=F32, precision=lax.Precision.HIGHEST)
        o_ref[0, :, c * LANES:(c + 1) * LANES] = inner + carry
        carry = carry + jnp.sum(x, axis=1, keepdims=True)


def forget_suffix(lf_t):
    bsz, nh, s = lf_t.shape
    return pl.pallas_call(
        _suffix_kernel,
        grid=(bsz,),
        in_specs=[pl.BlockSpec((1, nh, s), lambda b: (b, 0, 0))],
        out_specs=pl.BlockSpec((1, nh, s), lambda b: (b, 0, 0)),
        out_shape=jax.ShapeDtypeStruct((bsz, nh, s), F32),
        compiler_params=_cparams(("parallel",)),
        name="forget_suffix",
    )(lf_t)


def _fox_prompt_kernel(q_ref, k_ref, v_ref, sq_ref, sk_ref, o_ref, acc_ref, m_ref, l_ref, *, tq):
    i = pl.program_id(1)
    j = pl.program_id(2)
    scale = HD ** -0.5

    @pl.when(j == 0)
    def _():
        acc_ref[...] = jnp.zeros_like(acc_ref)
        m_ref[...] = jnp.full_like(m_ref, NEG_INF)
        l_ref[...] = jnp.zeros_like(l_ref)

    @pl.when(j <= i)
    def _():
        row = lax.broadcasted_iota(I32, (tq, tq), 0)
        col = lax.broadcasted_iota(I32, (tq, tq), 1)
        visible = (col <= row) | (j < i)
        sq = sq_ref[...]
        sk = sk_ref[0]
        for h in range(HEADS):
            sl = slice(h * HD, (h + 1) * HD)
            sqb = jnp.broadcast_to(sq[:, h:h + 1], (tq, LANES))
            s = _dot_nt(q_ref[:, sl].astype(BF16), k_ref[:, sl].astype(BF16)) * scale + sk[h:h + 1, :]
            s = jnp.where(visible, s, NEG_INF)
            m_prev = m_ref[h]
            m_new = jnp.maximum(m_prev, jnp.max(s, axis=1, keepdims=True) - sqb)
            alpha = jnp.exp(m_prev - m_new)
            p = jnp.exp(s - pltpu.repeat(m_new + sqb, tq // LANES, axis=1))
            l_ref[h] = alpha * l_ref[h] + jnp.sum(p, axis=1, keepdims=True)
            acc_ref[:, sl] = alpha * acc_ref[:, sl] + _dot(p.astype(BF16), v_ref[:, sl].astype(BF16))
            m_ref[h] = m_new

    @pl.when(j == i)
    def _():
        for h in range(HEADS):
            sl = slice(h * HD, (h + 1) * HD)
            o_ref[:, sl] = acc_ref[:, sl] / l_ref[h]


def fox_prompt(q, k, v, suf_tok, suf_t, bsz, seq, tq=512):
    tq = min(tq, seq)
    nq = seq // tq
    t = bsz * seq
    kv_spec = pl.BlockSpec((tq, D), lambda b, i, j: (b * nq + jnp.minimum(j, i), 0))
    return pl.pallas_call(
        functools.partial(_fox_prompt_kernel, tq=tq),
        grid=(bsz, nq, nq),
        in_specs=[pl.BlockSpec((tq, D), lambda b, i, j: (b * nq + i, 0)),
                  kv_spec, kv_spec,
                  pl.BlockSpec((tq, HEADS), lambda b, i, j: (b * nq + i, 0)),
                  pl.BlockSpec((1, HEADS, tq), lambda b, i, j: (b, 0, jnp.minimum(j, i)))],
        out_specs=pl.BlockSpec((tq, D), lambda b, i, j: (b * nq + i, 0)),
        out_shape=jax.ShapeDtypeStruct((t, D), F32),
        scratch_shapes=[pltpu.VMEM((tq, D), F32), pltpu.VMEM((HEADS, tq, LANES), F32),
                        pltpu.VMEM((HEADS, tq, LANES), F32)],
        compiler_params=_cparams(("parallel", "parallel", "arbitrary")),
        name="fox_prompt",
    )(q, k, v, suf_tok, suf_t)


def _page_suffix_kernel(lf_ref, suf_ref, tot_ref):
    w = lf_ref.shape[1]
    row = lax.broadcasted_iota(I32, (w, w), 0)
    col = lax.broadcasted_iota(I32, (w, w), 1)
    same_head = (row % HEADS) == (col % HEADS)
    later = (same_head & (row > col)).astype(BF16)
    whole = same_head.astype(BF16)
    x = lf_ref[...]
    hi = x.astype(BF16)
    r1 = x - hi.astype(F32)
    mid = r1.astype(BF16)
    lo = (r1 - mid.astype(F32)).astype(BF16)
    suf_ref[...] = _dot(hi, later) + _dot(mid, later) + _dot(lo, later)
    tot_ref[...] = _dot(hi, whole) + _dot(mid, whole) + _dot(lo, whole)


def page_suffix(lf_flat, tr=512):
    r, w = lf_flat.shape
    tr = math.gcd(r, tr)
    spec = pl.BlockSpec((tr, w), lambda i: (i, 0))
    return pl.pallas_call(
        _page_suffix_kernel,
        grid=(r // tr,),
        in_specs=[spec],
        out_specs=[spec, spec],
        out_shape=[jax.ShapeDtypeStruct((r, w), F32), jax.ShapeDtypeStruct((r, w), F32)],
        compiler_params=_cparams(("parallel",)),
        name="page_suffix",
    )(lf_flat)


def _fox_decode_kernel(pt_ref, q_ref, kn_ref, vn_ref, lfn_ref, *refs, pg):
    page_refs = refs[:4 * pg]
    o_ref, acc_ref, m_ref, l_ref, r_ref = refs[4 * pg:]
    j = pl.program_id(1)
    scale = HD ** -0.5
    w = r_ref.shape[1]
    q8 = q_ref[0]

    @pl.when(j == 0)
    def _():
        s_new = jnp.sum(q8 * kn_ref[0], axis=1, keepdims=True) * scale
        m_ref[...] = jnp.broadcast_to(s_new, m_ref.shape)
        l_ref[...] = jnp.ones_like(l_ref)
        acc_ref[...] = vn_ref[0]
        r_ref[...] = lfn_ref[0]

    run = r_ref[...]
    bias, ks, vs = [], [], []
    for i in range(pg):
        k_ref, v_ref, suf_ref, tot_ref = page_refs[4 * i:4 * i + 4]
        bias.append(suf_ref[0] + run)
        run = run + tot_ref[0]
        ks.append(k_ref[0].reshape(w, HD).astype(BF16))
        vs.append(v_ref[0].reshape(w, HD).astype(BF16))
    r_ref[...] = run
    own = (lax.broadcasted_iota(I32, (HEADS, pg * w), 1) % HEADS) == lax.broadcasted_iota(I32, (HEADS, pg * w), 0)
    s = _dot_nt(q8.astype(BF16), jnp.concatenate(ks, axis=0)) * scale + jnp.concatenate(bias, axis=1)
    s = jnp.where(own, s, NEG_INF)
    m_prev = m_ref[:, 0:1]
    m_new = jnp.maximum(m_prev, jnp.max(s, axis=1, keepdims=True))
    alpha = jnp.exp(m_prev - m_new)
    p = jnp.exp(s - m_new)
    l_ref[...] = alpha * l_ref[...] + jnp.sum(p, axis=1, keepdims=True)
    acc_ref[...] = alpha * acc_ref[...] + _dot(p.astype(BF16), jnp.concatenate(vs, axis=0))
    m_ref[...] = jnp.broadcast_to(m_new, m_ref.shape)

    @pl.when(j == pl.num_programs(1) - 1)
    def _():
        o_ref[0] = acc_ref[...] / l_ref[...]


def fox_decode(q, k, v, lf_new, cache_k, cache_v, cache_logf, page_table, pages_per_step=4):
    nb, n_pages = page_table.shape
    n_pool, page = cache_k.shape[:2]
    w = page * HEADS
    pg = math.gcd(n_pages, pages_per_step)
    suf_in, tot = page_suffix(cache_logf.reshape(n_pool, w))
    suf_in = suf_in.reshape(n_pool, 1, w)
    tot = tot.reshape(n_pool, 1, w)
    lfn = jnp.tile(lf_new, (1, page)).reshape(nb, 1, w)
    tok = lambda b, j, pt: (b, 0, 0)
    in_specs = [pl.BlockSpec((1, HEADS, HD), tok), pl.BlockSpec((1, HEADS, HD), tok), pl.BlockSpec((1, HEADS, HD), tok),
                pl.BlockSpec((1, 1, w), tok)]
    operands = []
    for i in range(pg):
        sel4 = lambda b, j, pt, i=i: (pt[b, n_pages - 1 - (j * pg + i)], 0, 0, 0)
        sel3 = lambda b, j, pt, i=i: (pt[b, n_pages - 1 - (j * pg + i)], 0, 0)
        in_specs += [pl.BlockSpec((1, page, HEADS, HD), sel4), pl.BlockSpec((1, page, HEADS, HD), sel4),
                     pl.BlockSpec((1, 1, w), sel3), pl.BlockSpec((1, 1, w), sel3)]
        operands += [cache_k, cache_v, suf_in, tot]
    grid_spec = pltpu.PrefetchScalarGridSpec(
        num_scalar_prefetch=1,
        grid=(nb, n_pages // pg),
        in_specs=in_specs,
        out_specs=pl.BlockSpec((1, HEADS, HD), tok),
        scratch_shapes=[pltpu.VMEM((HEADS, HD), F32), pltpu.VMEM((HEADS, LANES), F32),
                        pltpu.VMEM((HEADS, LANES), F32), pltpu.VMEM((1, w), F32)],
    )
    return pl.pallas_call(
        functools.partial(_fox_decode_kernel, pg=pg),
        grid_spec=grid_spec,
        out_shape=jax.ShapeDtypeStruct((nb, HEADS, HD), F32),
        compiler_params=_cparams(("parallel", "arbitrary")),
        name="fox_decode",
    )(page_table, q, k, v, lfn, *operands)


PICKS = PEER_HEADS * PEER_TOPK
EXPERT_BITS = 14
PAIR_ROWS = 8
TILE_ROWS = 2 * PAIR_ROWS
GROWS = PICKS * TILE_ROWS


def _take_topk(s, tie, k, put):
    big = 3.0e38
    for r in range(k):
        m = jnp.max(s, axis=0, keepdims=True)
        pos = jnp.min(jnp.where(s == m, tie, big), axis=0, keepdims=True)
        put(r, m, pos)
        s = jnp.where(tie == pos, NEG_INF, s)


def _candidate_bins(kk):
    full, small = [], []
    for a in range(kk):
        nb = kk // (a + 1)
        b0 = 0
        while nb - b0 >= 8:
            full.append([(a, b0, 0, 8)])
            b0 += 8
        if nb > b0:
            small.append((a, b0, nb - b0))
    bins = []
    for a, b0, rows in sorted(small, key=lambda piece: -piece[2]):
        for group in bins:
            used = group[-1][2] + group[-1][3]
            if used + rows <= 8:
                group.append((a, b0, used, rows))
                break
        else:
            bins.append([(a, b0, 0, rows)])
    return full + bins


def _peer_select_kernel(x_ref, wq_ref, sk_ref, idx_ref, off_ref, g_ref, sc_ref, sv_ref, si_ref, e_ref, gg_ref):
    tb = x_ref.shape[0]
    kk = PEER_TOPK
    nk = PEER_NKEYS
    lw = LANES
    q = _dot(x_ref[...].astype(BF16), wq_ref[...]).astype(BF16)
    key_id = lax.broadcasted_iota(I32, (nk, lw), 0).astype(F32)
    sub8 = lax.broadcasted_iota(I32, (8, lw), 0)
    sub8f = sub8.astype(F32)
    bins = _candidate_bins(kk)

    for h in range(PEER_HEADS):
        for c in range(2):
            qc = q[:, (2 * h + c) * nk:(2 * h + c + 1) * nk]
            sc_ref[c] = _dot_nt(sk_ref[c], qc)
        for part in range(tb // lw):
            ls = slice(part * lw, (part + 1) * lw)
            for c in range(2):
                def put1(r, m, pos, c=c):
                    sv_ref[c, r:r + 1, ls] = m
                    si_ref[c, r:r + 1, ls] = pos

                _take_topk(sc_ref[c, :, ls], key_id, kk, put1)

            cand, ckey = [], []
            for pieces in bins:
                va = ia = vb = ib = flat = None
                used = 0
                for a, b0, start, rows in pieces:
                    ra = jnp.broadcast_to(sv_ref[0, a:a + 1, ls], (8, lw))
                    ri = jnp.broadcast_to(si_ref[0, a:a + 1, ls], (8, lw))
                    rb = sv_ref[1, b0:b0 + 8, ls]
                    rj = si_ref[1, b0:b0 + 8, ls]
                    rf = sub8f + float(a * kk + b0 - start)
                    if start:
                        rb = pltpu.roll(rb, start, axis=0)
                        rj = pltpu.roll(rj, start, axis=0)
                        here = sub8 >= start
                        va, ia, vb = jnp.where(here, ra, va), jnp.where(here, ri, ia), jnp.where(here, rb, vb)
                        ib, flat = jnp.where(here, rj, ib), jnp.where(here, rf, flat)
                    else:
                        va, ia, vb, ib, flat = ra, ri, rb, rj, rf
                    used = start + rows
                c8 = va + vb
                cand.append(c8 if used == 8 else jnp.where(sub8 < used, c8, NEG_INF))
                ckey.append(flat * float(2 ** EXPERT_BITS) + (ia * float(nk) + ib))
            cand = jnp.concatenate(cand, axis=0)
            ckey = jnp.concatenate(ckey, axis=0)

            def put2(r, m, pos, h=h):
                gg_ref[h * kk + r:h * kk + r + 1, ls] = m
                e_ref[h * kk + r:h * kk + r + 1, ls] = pos

            _take_topk(cand, ckey, kk, put2)
            cv = gg_ref[h * kk:(h + 1) * kk, ls]
            ex = jnp.exp(cv - cv[0:1, :])
            gg_ref[h * kk:(h + 1) * kk, ls] = ex / jnp.sum(ex, axis=0, keepdims=True)

    expert = e_ref[...].T.astype(I32) & (2 ** EXPERT_BITS - 1)
    idx_ref[...] = expert
    off_ref[...] = (expert >> 1) * PAIR_ROWS
    g_ref[...] = gg_ref[...].T


def peer_select(x, wq_bf, sk_bf, tb=256):
    m, n = x.shape
    nq = wq_bf.shape[1]
    tb = min(tb, m)
    out_spec = pl.BlockSpec((tb, PICKS), lambda i: (i, 0))
    return pl.pallas_call(
        _peer_select_kernel,
        grid=(m // tb,),
        in_specs=[pl.BlockSpec((tb, n), lambda i: (i, 0)),
                  pl.BlockSpec((n, nq), lambda i: (0, 0)),
                  pl.BlockSpec((2, PEER_NKEYS, PEER_NKEYS), lambda i: (0, 0, 0))],
        out_specs=[out_spec, out_spec, out_spec],
        out_shape=[jax.ShapeDtypeStruct((m, PICKS), I32), jax.ShapeDtypeStruct((m, PICKS), I32),
                   jax.ShapeDtypeStruct((m, PICKS), F32)],
        scratch_shapes=[pltpu.VMEM((2, PEER_NKEYS, tb), F32),
                        pltpu.VMEM((2, PEER_TOPK, tb), F32), pltpu.VMEM((2, PEER_TOPK, tb), F32),
                        pltpu.VMEM((PICKS, tb), F32), pltpu.VMEM((PICKS, tb), F32)],
        compiler_params=_cparams(("parallel",)),
        name="peer_select",
    )(x, wq_bf, sk_bf)


def _diag_mask():
    sub = lax.broadcasted_iota(I32, (8, GROWS), 0)
    lane = lax.broadcasted_iota(I32, (8, GROWS), 1)
    return (lane % TILE_ROWS) // 2 == sub


def _gather_pairs(off_smem, tab_ref, t):
    tiles = []
    for p in range(PICKS):
        off = pl.multiple_of(off_smem[t, p], PAIR_ROWS)
        tiles.append(pltpu.bitcast(tab_ref[pl.ds(off, PAIR_ROWS), :], BF16))
    return jnp.concatenate(tiles, axis=0)


TOKEN_UNROLL = 8


def _peer_up_kernel(off_smem, x_ref, idxv_ref, gate_ref, tab_ref, ecomp_ref, w_ref, a_ref):
    tbu = x_ref.shape[0]
    mdiag = _diag_mask()

    def body(t, carry):
        g = _gather_pairs(off_smem, tab_ref, t)
        xr = x_ref[pl.ds(t, 1), :]
        x8 = jnp.concatenate([xr[:, s * LANES:(s + 1) * LANES] for s in range(D // LANES)], axis=0)
        res = _dot_nt(x8.astype(BF16), g)
        a_ref[pl.ds(t, 1), :] = jnp.sum(jnp.where(mdiag, res, 0.0), axis=0, keepdims=True)
        return carry

    lax.fori_loop(0, tbu, body, 0, unroll=TOKEN_UNROLL)
    hi, lo = _split_bf16(a_ref[...])
    both = _dot(hi, ecomp_ref[...]) + _dot(lo, ecomp_ref[...])
    odd = (idxv_ref[...] & 1) == 1
    a = jnp.where(odd, both[:, PICKS:], both[:, :PICKS])
    w_ref[...] = gate_ref[...] * (0.5 * a * (1.0 + lax.erf(a * (2.0 ** -0.5))))


def _peer_down_kernel(off_smem, w_ref, idxv_ref, tab_ref, eexp_ref, o_ref, wx_ref):
    tbu = w_ref.shape[0]
    mdiag = _diag_mask()
    w = w_ref[...]
    odd = (idxv_ref[...] & 1) == 1
    w01 = jnp.concatenate([jnp.where(odd, 0.0, w), jnp.where(odd, w, 0.0)], axis=1).astype(BF16)
    wx_ref[...] = _dot(w01, eexp_ref[...])

    def body(t, carry):
        g = _gather_pairs(off_smem, tab_ref, t)
        wm = jnp.where(mdiag, wx_ref[pl.ds(t, 1), :], 0.0).astype(BF16)
        o_ref[t] = _dot(wm, g)
        return carry

    lax.fori_loop(0, tbu, body, 0, unroll=TOKEN_UNROLL)


def _expand_matrix():
    r = lax.broadcasted_iota(I32, (2 * PICKS, GROWS), 0)
    k = lax.broadcasted_iota(I32, (2 * PICKS, GROWS), 1)
    return ((k // TILE_ROWS == r % PICKS) & (k % 2 == r // PICKS)).astype(BF16)


def _table_spec(tab):
    return pl.BlockSpec(tab.shape, lambda i: (0, 0), pipeline_mode=pl.Buffered(1))


def peer_up(off, idx, x, gate, tab, tbu=128):
    m = x.shape[0]
    tbu = min(tbu, m)
    ecomp = _expand_matrix().T
    return pl.pallas_call(
        _peer_up_kernel,
        grid=(m // tbu,),
        in_specs=[pl.BlockSpec((tbu, PICKS), lambda i: (i, 0), memory_space=pltpu.SMEM),
                  pl.BlockSpec((tbu, D), lambda i: (i, 0)),
                  pl.BlockSpec((tbu, PICKS), lambda i: (i, 0)),
                  pl.BlockSpec((tbu, PICKS), lambda i: (i, 0)),
                  _table_spec(tab),
                  pl.BlockSpec((GROWS, 2 * PICKS), lambda i: (0, 0))],
        out_specs=pl.BlockSpec((tbu, PICKS), lambda i: (i, 0)),
        out_shape=jax.ShapeDtypeStruct((m, PICKS), F32),
        scratch_shapes=[pltpu.VMEM((tbu, GROWS), F32)],
        compiler_params=_cparams(("arbitrary",)),
        name="peer_up",
    )(off, x, idx, gate, tab, ecomp)


def peer_down(off, idx, w, tab, tbu=128):
    m = w.shape[0]
    tbu = min(tbu, m)
    eexp = _expand_matrix()
    return pl.pallas_call(
        _peer_down_kernel,
        grid=(m // tbu,),
        in_specs=[pl.BlockSpec((tbu, PICKS), lambda i: (i, 0), memory_space=pltpu.SMEM),
                  pl.BlockSpec((tbu, PICKS), lambda i: (i, 0)),
                  pl.BlockSpec((tbu, PICKS), lambda i: (i, 0)),
                  _table_spec(tab),
                  pl.BlockSpec((2 * PICKS, GROWS), lambda i: (0, 0))],
        out_specs=pl.BlockSpec((tbu, 8, LANES), lambda i: (i, 0, 0)),
        out_shape=jax.ShapeDtypeStruct((m, 8, LANES), F32),
        scratch_shapes=[pltpu.VMEM((tbu, GROWS), F32)],
        compiler_params=_cparams(("arbitrary",)),
        name="peer_down",
    )(off, w, idx, tab, eexp).reshape(m, D)


def _pack_table(tab):
    e = tab.shape[0]
    bits = lax.bitcast_convert_type(tab.astype(BF16), jnp.uint16).astype(jnp.uint32).reshape(e // 2, 2, D)
    words = bits[:, 0, :] | (bits[:, 1, :] << 16)
    return words.reshape(e // 2 * PAIR_ROWS, LANES)


def peer(x, wq_bf, sk_bf, u_tab, v_tab):
    idx, off, gate = peer_select(x, wq_bf, sk_bf)
    w = peer_up(off, idx, x, gate, u_tab)
    return peer_down(off, idx, w, v_tab)


def _pad_cols(w, n=LANES):
    return jnp.pad(w, ((0, 0), (0, n - w.shape[1])))


def _pad_rows(a, rows):
    return jnp.pad(a, ((0, rows - a.shape[0]),) + ((0, 0),) * (a.ndim - 1))


def kernel(x_prompt, x_sample, cache_k, cache_v, cache_logf, state_C, state_n, state_m, page_table, p_prompt, p_sample, a_w_in, a_b_gate, a_gn_g, a_w_out, kv_ln_g, kv_ln_b, kv_w, kv_b_f, b_w_q, b_w_o, ln_mix_g, ln_mix_b, ln_ffn_g, ln_ffn_b, peer_w_q, peer_subkeys, peer_u, peer_v, ple_w_p, ple_w_g, ple_b_g):
    bsz, seq, _ = x_prompt.shape
    db = x_sample.shape[0]
    n_pool, page = cache_k.shape[:2]
    sdt = state_C.dtype
    ldt = cache_logf.dtype

    w_in_bf = a_w_in[0][:, :4 * D].astype(BF16)
    w_gate = _pad_cols(a_w_in[0][:, 4 * D:])
    b_gate = _pad_cols(a_b_gate[0].reshape(1, -1))
    w_out_bf = a_w_out[0].astype(BF16)
    wk_bf = kv_w[:, :D].astype(BF16)
    wv_bf = kv_w[:, D:2 * D].astype(BF16)
    wf = _pad_cols(kv_w[:, 2 * D:])
    bf = _pad_cols(kv_b_f.reshape(1, -1))
    bwq_bf = b_w_q[0].astype(BF16)
    bwo_bf = b_w_o[0].astype(BF16)
    pwq_bf = peer_w_q.astype(BF16)
    sk_bf = peer_subkeys.astype(BF16)
    u_tabs = [_pack_table(peer_u[i]) for i in range(DEPTH)]
    v_tabs = [_pack_table(peer_v[i]) for i in range(DEPTH)]
    wg_bf = ple_w_g.astype(BF16)
    wp_bf = ple_w_p.astype(BF16)

    def channel_mix(i, x_mid, p):
        po = peer(x_mid, pwq_bf[i], sk_bf[i], u_tabs[i], v_tabs[i])
        return ln_ple(x_mid, po, ln_ffn_g[i], ln_ffn_b[i], p, wg_bf[i], ple_b_g[i], wp_bf[i])

    def shared_kv(x):
        k, v, lfp = ln_kv(x, kv_ln_g, kv_ln_b, wk_bf, wv_bf, wf, bf)
        return k, v, lfp[:, :HEADS]

    t = bsz * seq
    x = x_prompt.reshape(t, D)
    p = p_prompt.reshape(DEPTH, t, -1)
    qkvo = matmul(x, w_in_bf)
    gates = gates_matmul(x, w_gate, b_gate, HEADS)
    y_pre, c_p, n_p, m_p = mlstm_prompt(qkvo, gates, a_gn_g[0], bsz, seq)
    x = matmul_ln(y_pre, w_out_bf, x, ln_mix_g[0], ln_mix_b[0])
    x = channel_mix(0, x, p[0])

    k_p, v_p, lf_p = shared_kv(x)
    lf_t = lf_p.reshape(bsz, seq, HEADS).transpose(0, 2, 1)
    suf_t = forget_suffix(lf_t)
    suf_tok = suf_t.transpose(0, 2, 1).reshape(t, HEADS)
    q = matmul(x, bwq_bf)
    o = fox_prompt(q, k_p, v_p, suf_tok, suf_t, bsz, seq)
    x = matmul_ln(o, bwo_bf, x, ln_mix_g[1], ln_mix_b[1])
    y_prompt = channel_mix(1, x, p[1]).reshape(bsz, seq, D)

    xs = _pad_rows(x_sample.reshape(db, D), DEC_PAD)
    ps = jnp.pad(p_sample.reshape(DEPTH, db, -1), ((0, 0), (0, DEC_PAD - db), (0, 0)))
    qkvo_s = matmul(xs, w_in_bf)
    gates_s = gates_matmul(xs, w_gate, b_gate, HEADS)
    y_s, c_s, n_s, m_s = mlstm_step(qkvo_s[:db].reshape(db, 1, 4 * D), gates_s[:db].reshape(db, 1, LANES),
                                    a_gn_g[0], state_C[0].astype(F32), state_n[0].astype(F32),
                                    state_m[0].astype(F32))
    xs = matmul_ln(_pad_rows(y_s.reshape(db, D), DEC_PAD), w_out_bf, xs, ln_mix_g[0], ln_mix_b[0])
    xs = channel_mix(0, xs, ps[0])

    k_s, v_s, lf_s = shared_kv(xs)
    q_s = matmul(xs, bwq_bf)
    o_s = fox_decode(q_s[:db].reshape(db, HEADS, HD), k_s[:db].reshape(db, HEADS, HD), v_s[:db].reshape(db, HEADS, HD),
                     lf_s[:db], cache_k, cache_v, cache_logf.astype(F32), page_table)
    xs = matmul_ln(_pad_rows(o_s.reshape(db, D), DEC_PAD), bwo_bf, xs, ln_mix_g[1], ln_mix_b[1])
    y_sample = channel_mix(1, xs, ps[1])[:db].reshape(db, 1, D)

    return (y_prompt, y_sample,
            k_p.reshape(bsz, seq, HEADS, HD), v_p.reshape(bsz, seq, HEADS, HD),
            lf_p.reshape(bsz, seq, HEADS).astype(ldt),
            c_p[None].astype(sdt), n_p[None].astype(sdt), m_p[:, :, 0][None].astype(sdt),
            k_s[:db].reshape(db, 1, HEADS, HD), v_s[:db].reshape(db, 1, HEADS, HD),
            lf_s[:db].reshape(db, 1, HEADS).astype(ldt),
            c_s[None].astype(sdt), n_s[None].astype(sdt), m_s[:, 0, :HEADS][None].astype(sdt))
```

```python
import functools
import math

import jax
import jax.numpy as jnp
from jax import lax
from jax.experimental import pallas as pl
from jax.experimental.pallas import tpu as pltpu

F32 = jnp.float32
BF16 = jnp.bfloat16
I32 = jnp.int32

D = 1024
HEADS = 8
HD = 128
DEPTH = 2
PEER_HEADS = 8
PEER_NKEYS = 128
PEER_TOPK = 16
ALPHA = (2.0 * DEPTH) ** 0.25
LN_EPS = 1e-5
HEAD_NORM_EPS = 1e-6
CHUNK = 128
DEC_PAD = 128

LANES = 128
VMEM_LIMIT = 56 * 1024 * 1024

NEG_INF = float("-inf")


def _cparams(sem, vmem=None):
    return pltpu.CompilerParams(dimension_semantics=sem, vmem_limit_bytes=vmem or VMEM_LIMIT)


def _layer_norm(z, g, b):
    mu = jnp.mean(z, axis=-1, keepdims=True)
    zc = z - mu
    var = jnp.mean(zc * zc, axis=-1, keepdims=True)
    return zc * lax.rsqrt(var + LN_EPS) * g + b


def _log_sigmoid(z):
    return jnp.minimum(z, 0.0) - jnp.log1p(jnp.exp(-jnp.abs(z)))


def _split_bf16(a):
    hi = a.astype(BF16)
    lo = (a - hi.astype(F32)).astype(BF16)
    return hi, lo


def _dot(a, b):
    return jnp.dot(a, b, preferred_element_type=F32)


def _dot_nt(a, b):
    return lax.dot_general(a, b, (((1,), (1,)), ((), ())), preferred_element_type=F32)


def _dot3(x, w):
    xh, xl = _split_bf16(x)
    wh, wl = _split_bf16(w)
    return _dot(xh, wh) + _dot(xl, wh) + _dot(xh, wl)


def _mm_kernel(x_ref, w_ref, o_ref, xb_ref):
    @pl.when(pl.program_id(1) == 0)
    def _():
        xb_ref[...] = x_ref[...].astype(BF16)

    o_ref[...] = _dot(xb_ref[...], w_ref[...])


def matmul(x, w_bf, tm=1024, tn=1024):
    m, k = x.shape
    n = w_bf.shape[1]
    tm = min(tm, m)
    tn = min(tn, n)
    return pl.pallas_call(
        _mm_kernel,
        grid=(m // tm, n // tn),
        in_specs=[pl.BlockSpec((tm, k), lambda i, j: (i, 0)),
                  pl.BlockSpec((k, tn), lambda i, j: (0, j))],
        out_specs=pl.BlockSpec((tm, tn), lambda i, j: (i, j)),
        out_shape=jax.ShapeDtypeStruct((m, n), F32),
        scratch_shapes=[pltpu.VMEM((tm, k), BF16)],
        compiler_params=_cparams(("parallel", "arbitrary")),
        name="matmul",
    )(x, w_bf)


def _gates_kernel(x_ref, w_ref, b_ref, o_ref, *, ls_start):
    z = _dot3(x_ref[...], w_ref[...]) + b_ref[...]
    col = lax.broadcasted_iota(I32, z.shape, 1)
    o_ref[...] = jnp.where(col >= ls_start, _log_sigmoid(z), z)


def gates_matmul(x, w_pad, b_pad, ls_start, tm=512):
    m, k = x.shape
    tm = min(tm, m)
    return pl.pallas_call(
        functools.partial(_gates_kernel, ls_start=ls_start),
        grid=(m // tm,),
        in_specs=[pl.BlockSpec((tm, k), lambda i: (i, 0)),
                  pl.BlockSpec((k, LANES), lambda i: (0, 0)),
                  pl.BlockSpec((1, LANES), lambda i: (0, 0))],
        out_specs=pl.BlockSpec((tm, LANES), lambda i: (i, 0)),
        out_shape=jax.ShapeDtypeStruct((m, LANES), F32),
        compiler_params=_cparams(("parallel",)),
        name="gates_matmul",
    )(x, w_pad, b_pad)


def _mm_ln_kernel(a_ref, w_ref, r_ref, g_ref, b_ref, o_ref):
    y = _dot(a_ref[...].astype(BF16), w_ref[...])
    o_ref[...] = _layer_norm(ALPHA * r_ref[...] + y, g_ref[...], b_ref[...])


def matmul_ln(a, w_bf, res, g, b, tm=512):
    m, k = a.shape
    n = w_bf.shape[1]
    tm = min(tm, m)
    return pl.pallas_call(
        _mm_ln_kernel,
        grid=(m // tm,),
        in_specs=[pl.BlockSpec((tm, k), lambda i: (i, 0)),
                  pl.BlockSpec((k, n), lambda i: (0, 0)),
                  pl.BlockSpec((tm, n), lambda i: (i, 0)),
                  pl.BlockSpec((1, n), lambda i: (0, 0)),
                  pl.BlockSpec((1, n), lambda i: (0, 0))],
        out_specs=pl.BlockSpec((tm, n), lambda i: (i, 0)),
        out_shape=jax.ShapeDtypeStruct((m, n), F32),
        compiler_params=_cparams(("parallel",)),
        name="matmul_ln",
    )(a, w_bf, res, g.reshape(1, n), b.reshape(1, n))


def _ln_ple_kernel(xm_ref, po_ref, g_ref, b_ref, p_ref, wg_ref, bg_ref, wp_ref, o_ref):
    x2 = _layer_norm(ALPHA * xm_ref[...] + po_ref[...], g_ref[...], b_ref[...])
    gate = jax.nn.sigmoid(_dot(x2.astype(BF16), wg_ref[...]) + bg_ref[...])
    emb = _dot(p_ref[...].astype(BF16), wp_ref[...])
    o_ref[...] = x2 + gate * emb


def ln_ple(xm, po, g, b, p, wg_bf, bg, wp_bf, tm=512):
    m, n = xm.shape
    kp = p.shape[1]
    tm = min(tm, m)
    row = lambda i: (i, 0)
    fixed = lambda i: (0, 0)
    return pl.pallas_call(
        _ln_ple_kernel,
        grid=(m // tm,),
        in_specs=[pl.BlockSpec((tm, n), row), pl.BlockSpec((tm, n), row),
                  pl.BlockSpec((1, n), fixed), pl.BlockSpec((1, n), fixed),
                  pl.BlockSpec((tm, kp), row),
                  pl.BlockSpec((n, n), fixed), pl.BlockSpec((1, n), fixed),
                  pl.BlockSpec((kp, n), fixed)],
        out_specs=pl.BlockSpec((tm, n), row),
        out_shape=jax.ShapeDtypeStruct((m, n), F32),
        compiler_params=_cparams(("parallel",)),
        name="ln_ple",
    )(xm, po, g.reshape(1, n), b.reshape(1, n), p, wg_bf, bg.reshape(1, n), wp_bf)


def _ln_kv_kernel(x_ref, g_ref, b_ref, wk_ref, wv_ref, wf_ref, bf_ref, k_ref, v_ref, lf_ref):
    st = _layer_norm(x_ref[...], g_ref[...], b_ref[...])
    sb = st.astype(BF16)
    k_ref[...] = _dot(sb, wk_ref[...])
    v_ref[...] = _dot(sb, wv_ref[...])
    lf_ref[...] = _log_sigmoid(_dot3(st, wf_ref[...]) + bf_ref[...])


def ln_kv(x, g, b, wk_bf, wv_bf, wf_pad, bf_pad, tm=512):
    m, n = x.shape
    tm = min(tm, m)
    row = lambda i: (i, 0)
    fixed = lambda i: (0, 0)
    return pl.pallas_call(
        _ln_kv_kernel,
        grid=(m // tm,),
        in_specs=[pl.BlockSpec((tm, n), row), pl.BlockSpec((1, n), fixed), pl.BlockSpec((1, n), fixed),
                  pl.BlockSpec((n, n), fixed), pl.BlockSpec((n, n), fixed),
                  pl.BlockSpec((n, LANES), fixed), pl.BlockSpec((1, LANES), fixed)],
        out_specs=[pl.BlockSpec((tm, n), row), pl.BlockSpec((tm, n), row), pl.BlockSpec((tm, LANES), row)],
        out_shape=[jax.ShapeDtypeStruct((m, n), F32), jax.ShapeDtypeStruct((m, n), F32),
                   jax.ShapeDtypeStruct((m, LANES), F32)],
        compiler_params=_cparams(("parallel",)),
        name="ln_kv",
    )(x, g.reshape(1, n), b.reshape(1, n), wk_bf, wv_bf, wf_pad, bf_pad)


def _head_norm_gate(hh, o_pre, gn):
    mu = jnp.mean(hh, axis=-1, keepdims=True)
    hc = hh - mu
    var = jnp.mean(hc * hc, axis=-1, keepdims=True)
    return jax.nn.sigmoid(o_pre) * (hc * lax.rsqrt(var + HEAD_NORM_EPS)) * gn


def _mlstm_prompt_kernel(q_ref, k_ref, v_ref, o_ref, g_ref, gn_ref,
                         y_ref, c_out_ref, n_out_ref, m_out_ref, c_ref, m_ref):
    c_idx = pl.program_id(1)
    L = CHUNK

    @pl.when(c_idx == 0)
    def _():
        c_ref[...] = jnp.zeros_like(c_ref)
        m_ref[...] = jnp.zeros_like(m_ref)

    row = lax.broadcasted_iota(I32, (L, L), 0)
    col = lax.broadcasted_iota(I32, (L, L), 1)
    causal = col <= row
    tril = causal.astype(F32)
    g = g_ref[...]
    cum = jnp.dot(tril, g, preferred_element_type=F32, precision=lax.Precision.HIGHEST)
    g_t = g.T
    cum_t = cum.T
    lane = lax.broadcasted_iota(I32, (L, HD), 1)
    ones_col = (lane == 0).astype(BF16)
    scale = HD ** -0.5

    for h in range(HEADS):
        sl = slice(h * HD, (h + 1) * HD)
        qb = q_ref[:, sl].astype(BF16)
        kb = (k_ref[:, sl] * scale).astype(BF16)
        v = v_ref[:, sl]
        vb = v.astype(BF16)
        li_c = g[:, h:h + 1]
        b_c = cum[:, HEADS + h:HEADS + h + 1]
        li_r = g_t[h:h + 1, :]
        b_r = cum_t[HEADS + h:HEADS + h + 1, :]
        m_prev = m_ref[h:h + 1, 0:1]
        dmat = jnp.where(causal, b_c - b_r + li_r, NEG_INF)
        inter = b_c + m_prev
        m_t = jnp.maximum(inter, jnp.max(dmat, axis=1, keepdims=True))
        w_inter = jnp.exp(inter - m_t)
        qk = _dot_nt(qb, kb) * jnp.exp(dmat - m_t)
        c_aug = c_ref[h]
        a_inter = _dot(qb, c_aug.astype(BF16))
        v_aug = jnp.concatenate([vb, ones_col], axis=1)
        a_intra = _dot(qk.astype(BF16), v_aug)
        num = w_inter * a_inter[:, :HD] + a_intra[:, :HD]
        den = w_inter * a_inter[:, HD:HD + 1] + a_intra[:, HD:HD + 1]
        hh = num / jnp.maximum(jnp.abs(den), jnp.exp(-m_t))
        y_ref[:, sl] = _head_norm_gate(hh, o_ref[:, sl], gn_ref[:, sl])

        b_last = b_c[L - 1:L, :]
        g_c = b_last - b_c + li_c
        m_new = jnp.maximum(b_last + m_prev, jnp.max(g_c, axis=0, keepdims=True))
        decay = jnp.exp(b_last + m_prev - m_new)
        w_c = jnp.exp(g_c - m_new)
        vw = jnp.concatenate([v * w_c, jnp.where(lane == 0, w_c, 0.0)], axis=1).astype(BF16)
        upd = lax.dot_general(kb, vw, (((0,), (0,)), ((), ())), preferred_element_type=F32)
        c_ref[h] = decay * c_aug + upd
        m_ref[h:h + 1, :] = jnp.broadcast_to(m_new, (1, LANES))

    @pl.when(c_idx == pl.num_programs(1) - 1)
    def _():
        for h in range(HEADS):
            c_aug = c_ref[h]
            c_out_ref[0, h] = c_aug[:, :HD]
            n_out_ref[0, h:h + 1, :] = c_aug[:, HD:].T[0:1, :]
        m_out_ref[0] = m_ref[...]


def mlstm_prompt(qkvo, gates, gn_g, bsz, seq):
    nc = seq // CHUNK
    t = bsz * seq

    def col_spec(gidx):
        return pl.BlockSpec((CHUNK, D), lambda b, c: (b * nc + c, gidx))

    return pl.pallas_call(
        _mlstm_prompt_kernel,
        grid=(bsz, nc),
        in_specs=[col_spec(0), col_spec(1), col_spec(2), col_spec(3),
                  pl.BlockSpec((CHUNK, LANES), lambda b, c: (b * nc + c, 0)),
                  pl.BlockSpec((1, D), lambda b, c: (0, 0))],
        out_specs=[pl.BlockSpec((CHUNK, D), lambda b, c: (b * nc + c, 0)),
                   pl.BlockSpec((1, HEADS, HD, HD), lambda b, c: (b, 0, 0, 0)),
                   pl.BlockSpec((1, HEADS, HD), lambda b, c: (b, 0, 0)),
                   pl.BlockSpec((1, HEADS, LANES), lambda b, c: (b, 0, 0))],
        out_shape=[jax.ShapeDtypeStruct((t, D), F32),
                   jax.ShapeDtypeStruct((bsz, HEADS, HD, HD), F32),
                   jax.ShapeDtypeStruct((bsz, HEADS, HD), F32),
                   jax.ShapeDtypeStruct((bsz, HEADS, LANES), F32)],
        scratch_shapes=[pltpu.VMEM((HEADS, HD, 2 * HD), F32), pltpu.VMEM((HEADS, LANES), F32)],
        compiler_params=_cparams(("parallel", "arbitrary")),
        name="mlstm_prompt",
    )(qkvo, qkvo, qkvo, qkvo, gates, gn_g.reshape(1, D))


def _mlstm_step_kernel(q_ref, k_ref, v_ref, o_ref, g_ref, gn_ref, c0_ref, n0_ref, m0_ref,
                       y_ref, c_out_ref, n_out_ref, m_out_ref):
    row = lax.broadcasted_iota(I32, (HD, HD), 0)
    col = lax.broadcasted_iota(I32, (HD, HD), 1)
    eye = row == col
    scale = HD ** -0.5
    g = g_ref[0]
    m0 = m0_ref[0]
    m_new_all = jnp.zeros((1, LANES), F32)
    lane = lax.broadcasted_iota(I32, (1, LANES), 1)

    def to_col(r):
        return jnp.sum(jnp.where(eye, r, 0.0), axis=1, keepdims=True)

    for h in range(HEADS):
        sl = slice(h * HD, (h + 1) * HD)
        q = q_ref[0][:, sl]
        ks = k_ref[0][:, sl] * scale
        v = v_ref[0][:, sl]
        li = g[:, h:h + 1]
        lf = g[:, HEADS + h:HEADS + h + 1]
        m_prev = m0[:, h:h + 1]
        c0 = c0_ref[0, h]
        n0 = n0_ref[0, h:h + 1, :]
        inter = lf + m_prev
        m_t = jnp.maximum(inter, li)
        w_inter = jnp.exp(inter - m_t)
        p = jnp.exp(li - m_t)
        qk = jnp.sum(q * ks, axis=1, keepdims=True) * p
        q_c = jnp.sum(to_col(q) * c0, axis=0, keepdims=True)
        q_n = jnp.sum(q * n0, axis=1, keepdims=True)
        num = w_inter * q_c + qk * v
        den = w_inter * q_n + qk
        hh = num / jnp.maximum(jnp.abs(den), jnp.exp(-m_t))
        y_ref[0, :, sl] = _head_norm_gate(hh, o_ref[0][:, sl], gn_ref[:, sl])
        decay = w_inter
        c_out_ref[0, h] = decay * c0 + to_col(p * ks) * v
        n_out_ref[0, h:h + 1, :] = decay * n0 + p * ks
        m_new_all = jnp.where(lane == h, m_t, m_new_all)
    m_out_ref[0] = m_new_all


def mlstm_step(qkvo3, gates3, gn_g, c0, n0, m0):
    nb = c0.shape[0]

    def col_spec(gidx):
        return pl.BlockSpec((1, 1, D), lambda b: (b, 0, gidx))

    return pl.pallas_call(
        _mlstm_step_kernel,
        grid=(nb,),
        in_specs=[col_spec(0), col_spec(1), col_spec(2), col_spec(3),
                  pl.BlockSpec((1, 1, LANES), lambda b: (b, 0, 0)),
                  pl.BlockSpec((1, D), lambda b: (0, 0)),
                  pl.BlockSpec((1, HEADS, HD, HD), lambda b: (b, 0, 0, 0)),
                  pl.BlockSpec((1, HEADS, HD), lambda b: (b, 0, 0)),
                  pl.BlockSpec((1, 1, HEADS), lambda b: (b, 0, 0))],
        out_specs=[pl.BlockSpec((1, 1, D), lambda b: (b, 0, 0)),
                   pl.BlockSpec((1, HEADS, HD, HD), lambda b: (b, 0, 0, 0)),
                   pl.BlockSpec((1, HEADS, HD), lambda b: (b, 0, 0)),
                   pl.BlockSpec((1, 1, LANES), lambda b: (b, 0, 0))],
        out_shape=[jax.ShapeDtypeStruct((nb, 1, D), F32),
                   jax.ShapeDtypeStruct((nb, HEADS, HD, HD), F32),
                   jax.ShapeDtypeStruct((nb, HEADS, HD), F32),
                   jax.ShapeDtypeStruct((nb, 1, LANES), F32)],
        compiler_params=_cparams(("parallel",)),
        name="mlstm_step",
    )(qkvo3, qkvo3, qkvo3, qkvo3, gates3, gn_g.reshape(1, D), c0, n0, m0.reshape(nb, 1, HEADS))


def _suffix_kernel(lf_ref, o_ref):
    s = lf_ref.shape[2]
    row = lax.broadcasted_iota(I32, (LANES, LANES), 0)
    col = lax.broadcasted_iota(I32, (LANES, LANES), 1)
    upper = (row > col).astype(F32)
    carry = jnp.zeros((HEADS, 1), F32)
    for c in reversed(range(s // LANES)):
        x = lf_ref[0, :, c * LANES:(c + 1) * LANES]
        inner = jnp.dot(x, upper, preferred_element_type=F32, precision=lax.Precision.HIGHEST)
        o_ref[0, :, c * LANES:(c + 1) * LANES] = inner + carry
        carry = carry + jnp.sum(x, axis=1, keepdims=True)


def forget_suffix(lf_t):
    bsz, nh, s = lf_t.shape
    return pl.pallas_call(
        _suffix_kernel,
        grid=(bsz,),
        in_specs=[pl.BlockSpec((1, nh, s), lambda b: (b, 0, 0))],
        out_specs=pl.BlockSpec((1, nh, s), lambda b: (b, 0, 0)),
        out_shape=jax.ShapeDtypeStruct((bsz, nh, s), F32),
        compiler_params=_cparams(("parallel",)),
        name="forget_suffix",
    )(lf_t)


def _fox_prompt_kernel(q_ref, k_ref, v_ref, sq_ref, sk_ref, o_ref, acc_ref, m_ref, l_ref, *, tq):
    i = pl.program_id(1)
    j = pl.program_id(2)
    scale = HD ** -0.5

    @pl.when(j == 0)
    def _():
        acc_ref[...] = jnp.zeros_like(acc_ref)
        m_ref[...] = jnp.full_like(m_ref, NEG_INF)
        l_ref[...] = jnp.zeros_like(l_ref)

    @pl.when(j <= i)
    def _():
        row = lax.broadcasted_iota(I32, (tq, tq), 0)
        col = lax.broadcasted_iota(I32, (tq, tq), 1)
        visible = (col <= row) | (j < i)
        sq = sq_ref[...]
        sk = sk_ref[0]
        for h in range(HEADS):
            sl = slice(h * HD, (h + 1) * HD)
            sqb = jnp.broadcast_to(sq[:, h:h + 1], (tq, LANES))
            s = _dot_nt(q_ref[:, sl].astype(BF16), k_ref[:, sl].astype(BF16)) * scale + sk[h:h + 1, :]
            s = jnp.where(visible, s, NEG_INF)
            m_prev = m_ref[h]
            m_new = jnp.maximum(m_prev, jnp.max(s, axis=1, keepdims=True) - sqb)
            alpha = jnp.exp(m_prev - m_new)
            p = jnp.exp(s - pltpu.repeat(m_new + sqb, tq // LANES, axis=1))
            l_ref[h] = alpha * l_ref[h] + jnp.sum(p, axis=1, keepdims=True)
            acc_ref[:, sl] = alpha * acc_ref[:, sl] + _dot(p.astype(BF16), v_ref[:, sl].astype(BF16))
            m_ref[h] = m_new

    @pl.when(j == i)
    def _():
        for h in range(HEADS):
            sl = slice(h * HD, (h + 1) * HD)
            o_ref[:, sl] = acc_ref[:, sl] / l_ref[h]


def fox_prompt(q, k, v, suf_tok, suf_t, bsz, seq, tq=512):
    tq = min(tq, seq)
    nq = seq // tq
    t = bsz * seq
    kv_spec = pl.BlockSpec((tq, D), lambda b, i, j: (b * nq + jnp.minimum(j, i), 0))
    return pl.pallas_call(
        functools.partial(_fox_prompt_kernel, tq=tq),
        grid=(bsz, nq, nq),
        in_specs=[pl.BlockSpec((tq, D), lambda b, i, j: (b * nq + i, 0)),
                  kv_spec, kv_spec,
                  pl.BlockSpec((tq, HEADS), lambda b, i, j: (b * nq + i, 0)),
                  pl.BlockSpec((1, HEADS, tq), lambda b, i, j: (b, 0, jnp.minimum(j, i)))],
        out_specs=pl.BlockSpec((tq, D), lambda b, i, j: (b * nq + i, 0)),
        out_shape=jax.ShapeDtypeStruct((t, D), F32),
        scratch_shapes=[pltpu.VMEM((tq, D), F32), pltpu.VMEM((HEADS, tq, LANES), F32),
                        pltpu.VMEM((HEADS, tq, LANES), F32)],
        compiler_params=_cparams(("parallel", "parallel", "arbitrary")),
        name="fox_prompt",
    )(q, k, v, suf_tok, suf_t)


def _page_suffix_kernel(lf_ref, suf_ref, tot_ref):
    w = lf_ref.shape[1]
    row = lax.broadcasted_iota(I32, (w, w), 0)
    col = lax.broadcasted_iota(I32, (w, w), 1)
    same_head = (row % HEADS) == (col % HEADS)
    later = (same_head & (row > col)).astype(BF16)
    whole = same_head.astype(BF16)
    x = lf_ref[...]
    hi = x.astype(BF16)
    r1 = x - hi.astype(F32)
    mid = r1.astype(BF16)
    lo = (r1 - mid.astype(F32)).astype(BF16)
    suf_ref[...] = _dot(hi, later) + _dot(mid, later) + _dot(lo, later)
    tot_ref[...] = _dot(hi, whole) + _dot(mid, whole) + _dot(lo, whole)


def page_suffix(lf_flat, tr=512):
    r, w = lf_flat.shape
    tr = math.gcd(r, tr)
    spec = pl.BlockSpec((tr, w), lambda i: (i, 0))
    return pl.pallas_call(
        _page_suffix_kernel,
        grid=(r // tr,),
        in_specs=[spec],
        out_specs=[spec, spec],
        out_shape=[jax.ShapeDtypeStruct((r, w), F32), jax.ShapeDtypeStruct((r, w), F32)],
        compiler_params=_cparams(("parallel",)),
        name="page_suffix",
    )(lf_flat)


def _fox_decode_kernel(pt_ref, q_ref, kn_ref, vn_ref, lfn_ref, *refs, pg):
    page_refs = refs[:4 * pg]
    o_ref, acc_ref, m_ref, l_ref, r_ref = refs[4 * pg:]
    j = pl.program_id(1)
    scale = HD ** -0.5
    w = r_ref.shape[1]
    q8 = q_ref[0]

    @pl.when(j == 0)
    def _():
        s_new = jnp.sum(q8 * kn_ref[0], axis=1, keepdims=True) * scale
        m_ref[...] = jnp.broadcast_to(s_new, m_ref.shape)
        l_ref[...] = jnp.ones_like(l_ref)
        acc_ref[...] = vn_ref[0]
        r_ref[...] = lfn_ref[0]

    run = r_ref[...]
    bias, ks, vs = [], [], []
    for i in range(pg):
        k_ref, v_ref, suf_ref, tot_ref = page_refs[4 * i:4 * i + 4]
        bias.append(suf_ref[0] + run)
        run = run + tot_ref[0]
        ks.append(k_ref[0].reshape(w, HD).astype(BF16))
        vs.append(v_ref[0].reshape(w, HD).astype(BF16))
    r_ref[...] = run
    own = (lax.broadcasted_iota(I32, (HEADS, pg * w), 1) % HEADS) == lax.broadcasted_iota(I32, (HEADS, pg * w), 0)
    s = _dot_nt(q8.astype(BF16), jnp.concatenate(ks, axis=0)) * scale + jnp.concatenate(bias, axis=1)
    s = jnp.where(own, s, NEG_INF)
    m_prev = m_ref[:, 0:1]
    m_new = jnp.maximum(m_prev, jnp.max(s, axis=1, keepdims=True))
    alpha = jnp.exp(m_prev - m_new)
    p = jnp.exp(s - m_new)
    l_ref[...] = alpha * l_ref[...] + jnp.sum(p, axis=1, keepdims=True)
    acc_ref[...] = alpha * acc_ref[...] + _dot(p.astype(BF16), jnp.concatenate(vs, axis=0))
    m_ref[...] = jnp.broadcast_to(m_new, m_ref.shape)

    @pl.when(j == pl.num_programs(1) - 1)
    def _():
        o_ref[0] = acc_ref[...] / l_ref[...]


def fox_decode(q, k, v, lf_new, cache_k, cache_v, cache_logf, page_table, pages_per_step=8):
    nb, n_pages = page_table.shape
    n_pool, page = cache_k.shape[:2]
    w = page * HEADS
    pg = math.gcd(n_pages, pages_per_step)
    suf_in, tot = page_suffix(cache_logf.reshape(n_pool, w))
    suf_in = suf_in.reshape(n_pool, 1, w)
    tot = tot.reshape(n_pool, 1, w)
    lfn = jnp.tile(lf_new, (1, page)).reshape(nb, 1, w)
    tok = lambda b, j, pt: (b, 0, 0)
    in_specs = [pl.BlockSpec((1, HEADS, HD), tok), pl.BlockSpec((1, HEADS, HD), tok), pl.BlockSpec((1, HEADS, HD), tok),
                pl.BlockSpec((1, 1, w), tok)]
    operands = []
    for i in range(pg):
        sel4 = lambda b, j, pt, i=i: (pt[b, n_pages - 1 - (j * pg + i)], 0, 0, 0)
        sel3 = lambda b, j, pt, i=i: (pt[b, n_pages - 1 - (j * pg + i)], 0, 0)
        in_specs += [pl.BlockSpec((1, page, HEADS, HD), sel4), pl.BlockSpec((1, page, HEADS, HD), sel4),
                     pl.BlockSpec((1, 1, w), sel3), pl.BlockSpec((1, 1, w), sel3)]
        operands += [cache_k, cache_v, suf_in, tot]
    grid_spec = pltpu.PrefetchScalarGridSpec(
        num_scalar_prefetch=1,
        grid=(nb, n_pages // pg),
        in_specs=in_specs,
        out_specs=pl.BlockSpec((1, HEADS, HD), tok),
        scratch_shapes=[pltpu.VMEM((HEADS, HD), F32), pltpu.VMEM((HEADS, LANES), F32),
                        pltpu.VMEM((HEADS, LANES), F32), pltpu.VMEM((1, w), F32)],
    )
    return pl.pallas_call(
        functools.partial(_fox_decode_kernel, pg=pg),
        grid_spec=grid_spec,
        out_shape=jax.ShapeDtypeStruct((nb, HEADS, HD), F32),
        compiler_params=_cparams(("parallel", "arbitrary")),
        name="fox_decode",
    )(page_table, q, k, v, lfn, *operands)


PICKS = PEER_HEADS * PEER_TOPK
EXPERT_BITS = 14
PAIR_ROWS = 8
TILE_ROWS = 2 * PAIR_ROWS
GROWS = PICKS * TILE_ROWS


def _take_topk(s, tie, k, put):
    big = 3.0e38
    for r in range(k):
        m = jnp.max(s, axis=0, keepdims=True)
        pos = jnp.min(jnp.where(s == m, tie, big), axis=0, keepdims=True)
        put(r, m, pos)
        s = jnp.where(tie == pos, NEG_INF, s)


def _take_topk_keys(s, k, put):
    g = s.shape[0] // 8
    base = lax.broadcasted_iota(I32, (8, s.shape[1]), 0).astype(F32)
    vals = [s[8 * v:8 * v + 8, :] for v in range(g)]
    ids = [base + float(8 * v) for v in range(g)]
    for rnd in range(g):
        for i in range(rnd % 2, g - 1, 2):
            swap = vals[i + 1] > vals[i]
            vals[i], vals[i + 1] = jnp.where(swap, vals[i + 1], vals[i]), jnp.where(swap, vals[i], vals[i + 1])
            ids[i], ids[i + 1] = jnp.where(swap, ids[i + 1], ids[i]), jnp.where(swap, ids[i], ids[i + 1])
    big = 3.0e38
    for r in range(k):
        m = jnp.max(vals[0], axis=0, keepdims=True)
        pos = jnp.min(jnp.where(vals[0] == m, ids[0], big), axis=0, keepdims=True)
        put(r, m, pos)
        taken = ids[0] == pos
        last = min(g, k) - 1 - r
        for v in range(last):
            vals[v] = jnp.where(taken, vals[v + 1], vals[v])
            ids[v] = jnp.where(taken, ids[v + 1], ids[v])
        if last >= 0:
            vals[last] = jnp.where(taken, NEG_INF, vals[last])


def _candidate_bins(kk):
    full, small = [], []
    for a in range(kk):
        nb = kk // (a + 1)
        b0 = 0
        while nb - b0 >= 8:
            full.append([(a, b0, 0, 8)])
            b0 += 8
        if nb > b0:
            small.append((a, b0, nb - b0))
    bins = []
    for a, b0, rows in sorted(small, key=lambda piece: -piece[2]):
        for group in bins:
            used = group[-1][2] + group[-1][3]
            if used + rows <= 8:
                group.append((a, b0, used, rows))
                break
        else:
            bins.append([(a, b0, 0, rows)])
    return full + bins


def _peer_select_kernel(x_ref, wq_ref, sk_ref, idx_ref, off_ref, g_ref, sc_ref, sv_ref, si_ref, e_ref, gg_ref):
    tb = x_ref.shape[0]
    kk = PEER_TOPK
    nk = PEER_NKEYS
    lw = LANES
    q = _dot(x_ref[...].astype(BF16), wq_ref[...]).astype(BF16)
    sub8 = lax.broadcasted_iota(I32, (8, lw), 0)
    sub8f = sub8.astype(F32)
    bins = _candidate_bins(kk)

    for h in range(PEER_HEADS):
        for c in range(2):
            qc = q[:, (2 * h + c) * nk:(2 * h + c + 1) * nk]
            sc_ref[c] = _dot_nt(sk_ref[c], qc)
        for part in range(tb // lw):
            ls = slice(part * lw, (part + 1) * lw)
            for c in range(2):
                def put1(r, m, pos, c=c):
                    sv_ref[c, r:r + 1, ls] = m
                    si_ref[c, r:r + 1, ls] = pos

                _take_topk_keys(sc_ref[c, :, ls], kk, put1)

            cand, ckey = [], []
            for pieces in bins:
                va = ia = vb = ib = flat = None
                used = 0
                for a, b0, start, rows in pieces:
                    ra = jnp.broadcast_to(sv_ref[0, a:a + 1, ls], (8, lw))
                    ri = jnp.broadcast_to(si_ref[0, a:a + 1, ls], (8, lw))
                    rb = sv_ref[1, b0:b0 + 8, ls]
                    rj = si_ref[1, b0:b0 + 8, ls]
                    rf = sub8f + float(a * kk + b0 - start)
                    if start:
                        rb = pltpu.roll(rb, start, axis=0)
                        rj = pltpu.roll(rj, start, axis=0)
                        here = sub8 >= start
                        va, ia, vb = jnp.where(here, ra, va), jnp.where(here, ri, ia), jnp.where(here, rb, vb)
                        ib, flat = jnp.where(here, rj, ib), jnp.where(here, rf, flat)
                    else:
                        va, ia, vb, ib, flat = ra, ri, rb, rj, rf
                    used = start + rows
                c8 = va + vb
                cand.append(c8 if used == 8 else jnp.where(sub8 < used, c8, NEG_INF))
                ckey.append(flat * float(2 ** EXPERT_BITS) + (ia * float(nk) + ib))
            cand = jnp.concatenate(cand, axis=0)
            ckey = jnp.concatenate(ckey, axis=0)

            def put2(r, m, pos, h=h):
                gg_ref[h * kk + r:h * kk + r + 1, ls] = m
                e_ref[h * kk + r:h * kk + r + 1, ls] = pos

            _take_topk(cand, ckey, kk, put2)
            cv = gg_ref[h * kk:(h + 1) * kk, ls]
            ex = jnp.exp(cv - cv[0:1, :])
            gg_ref[h * kk:(h + 1) * kk, ls] = ex / jnp.sum(ex, axis=0, keepdims=True)

    expert = e_ref[...].T.astype(I32) & (2 ** EXPERT_BITS - 1)
    idx_ref[...] = expert
    off_ref[...] = (expert >> 1) * PAIR_ROWS
    g_ref[...] = gg_ref[...].T


def peer_select(x, wq_bf, sk_bf, tb=256):
    m, n = x.shape
    nq = wq_bf.shape[1]
    tb = min(tb, m)
    out_spec = pl.BlockSpec((tb, PICKS), lambda i: (i, 0))
    return pl.pallas_call(
        _peer_select_kernel,
        grid=(m // tb,),
        in_specs=[pl.BlockSpec((tb, n), lambda i: (i, 0)),
                  pl.BlockSpec((n, nq), lambda i: (0, 0)),
                  pl.BlockSpec((2, PEER_NKEYS, PEER_NKEYS), lambda i: (0, 0, 0))],
        out_specs=[out_spec, out_spec, out_spec],
        out_shape=[jax.ShapeDtypeStruct((m, PICKS), I32), jax.ShapeDtypeStruct((m, PICKS), I32),
                   jax.ShapeDtypeStruct((m, PICKS), F32)],
        scratch_shapes=[pltpu.VMEM((2, PEER_NKEYS, tb), F32),
                        pltpu.VMEM((2, PEER_TOPK, tb), F32), pltpu.VMEM((2, PEER_TOPK, tb), F32),
                        pltpu.VMEM((PICKS, tb), F32), pltpu.VMEM((PICKS, tb), F32)],
        compiler_params=_cparams(("parallel",)),
        name="peer_select",
    )(x, wq_bf, sk_bf)


def _diag_mask():
    sub = lax.broadcasted_iota(I32, (8, GROWS), 0)
    lane = lax.broadcasted_iota(I32, (8, GROWS), 1)
    return (lane % TILE_ROWS) // 2 == sub


def _gather_pairs(off_smem, tab_ref, t):
    tiles = []
    for p in range(PICKS):
        off = pl.multiple_of(off_smem[t, p], PAIR_ROWS)
        tiles.append(pltpu.bitcast(tab_ref[pl.ds(off, PAIR_ROWS), :], BF16))
    return jnp.concatenate(tiles, axis=0)


TOKEN_UNROLL = 16


def _peer_up_kernel(off_smem, x_ref, idxv_ref, gate_ref, tab_ref, ecomp_ref, w_ref, a_ref):
    tbu = x_ref.shape[0]
    mdiag = _diag_mask()

    def body(t, carry):
        g = _gather_pairs(off_smem, tab_ref, t)
        xr = x_ref[pl.ds(t, 1), :]
        x8 = jnp.concatenate([xr[:, s * LANES:(s + 1) * LANES] for s in range(D // LANES)], axis=0)
        res = _dot_nt(x8.astype(BF16), g)
        a_ref[pl.ds(t, 1), :] = jnp.sum(jnp.where(mdiag, res, 0.0), axis=0, keepdims=True)
        return carry

    lax.fori_loop(0, tbu, body, 0, unroll=TOKEN_UNROLL)
    hi, lo = _split_bf16(a_ref[...])
    both = _dot(hi, ecomp_ref[...]) + _dot(lo, ecomp_ref[...])
    odd = (idxv_ref[...] & 1) == 1
    a = jnp.where(odd, both[:, PICKS:], both[:, :PICKS])
    w_ref[...] = gate_ref[...] * (0.5 * a * (1.0 + lax.erf(a * (2.0 ** -0.5))))


def _peer_down_kernel(off_smem, w_ref, idxv_ref, tab_ref, eexp_ref, o_ref, wx_ref):
    tbu = w_ref.shape[0]
    mdiag = _diag_mask()
    w = w_ref[...]
    odd = (idxv_ref[...] & 1) == 1
    w01 = jnp.concatenate([jnp.where(odd, 0.0, w), jnp.where(odd, w, 0.0)], axis=1).astype(BF16)
    wx_ref[...] = _dot(w01, eexp_ref[...])

    def body(t, carry):
        g = _gather_pairs(off_smem, tab_ref, t)
        wm = jnp.where(mdiag, wx_ref[pl.ds(t, 1), :], 0.0).astype(BF16)
        o_ref[t] = _dot(wm, g)
        return carry

    lax.fori_loop(0, tbu, body, 0, unroll=TOKEN_UNROLL)


def _expand_matrix():
    r = lax.broadcasted_iota(I32, (2 * PICKS, GROWS), 0)
    k = lax.broadcasted_iota(I32, (2 * PICKS, GROWS), 1)
    return ((k // TILE_ROWS == r % PICKS) & (k % 2 == r // PICKS)).astype(BF16)


def _table_spec(tab):
    return pl.BlockSpec(tab.shape, lambda i: (0, 0), pipeline_mode=pl.Buffered(1))


def peer_up(off, idx, x, gate, tab, tbu=128):
    m = x.shape[0]
    tbu = min(tbu, m)
    ecomp = _expand_matrix().T
    return pl.pallas_call(
        _peer_up_kernel,
        grid=(m // tbu,),
        in_specs=[pl.BlockSpec((tbu, PICKS), lambda i: (i, 0), memory_space=pltpu.SMEM),
                  pl.BlockSpec((tbu, D), lambda i: (i, 0)),
                  pl.BlockSpec((tbu, PICKS), lambda i: (i, 0)),
                  pl.BlockSpec((tbu, PICKS), lambda i: (i, 0)),
                  _table_spec(tab),
                  pl.BlockSpec((GROWS, 2 * PICKS), lambda i: (0, 0))],
        out_specs=pl.BlockSpec((tbu, PICKS), lambda i: (i, 0)),
        out_shape=jax.ShapeDtypeStruct((m, PICKS), F32),
        scratch_shapes=[pltpu.VMEM((tbu, GROWS), F32)],
        compiler_params=_cparams(("arbitrary",)),
        name="peer_up",
    )(off, x, idx, gate, tab, ecomp)


def peer_down(off, idx, w, tab, tbu=128):
    m = w.shape[0]
    tbu = min(tbu, m)
    eexp = _expand_matrix()
    return pl.pallas_call(
        _peer_down_kernel,
        grid=(m // tbu,),
        in_specs=[pl.BlockSpec((tbu, PICKS), lambda i: (i, 0), memory_space=pltpu.SMEM),
                  pl.BlockSpec((tbu, PICKS), lambda i: (i, 0)),
                  pl.BlockSpec((tbu, PICKS), lambda i: (i, 0)),
                  _table_spec(tab),
                  pl.BlockSpec((2 * PICKS, GROWS), lambda i: (0, 0))],
        out_specs=pl.BlockSpec((tbu, 8, LANES), lambda i: (i, 0, 0)),
        out_shape=jax.ShapeDtypeStruct((m, 8, LANES), F32),
        scratch_shapes=[pltpu.VMEM((tbu, GROWS), F32)],
        compiler_params=_cparams(("arbitrary",)),
        name="peer_down",
    )(off, w, idx, tab, eexp).reshape(m, D)


def _pack_table(tab):
    e = tab.shape[0]
    bits = lax.bitcast_convert_type(tab.astype(BF16), jnp.uint16).astype(jnp.uint32).reshape(e // 2, 2, D)
    words = bits[:, 0, :] | (bits[:, 1, :] << 16)
    return words.reshape(e // 2 * PAIR_ROWS, LANES)


def peer(x, wq_bf, sk_bf, u_tab, v_tab):
    idx, off, gate = peer_select(x, wq_bf, sk_bf)
    w = peer_up(off, idx, x, gate, u_tab)
    return peer_down(off, idx, w, v_tab)


def _pad_cols(w, n=LANES):
    return jnp.pad(w, ((0, 0), (0, n - w.shape[1])))


def _pad_rows(a, rows):
    return jnp.pad(a, ((0, rows - a.shape[0]),) + ((0, 0),) * (a.ndim - 1))


def kernel(x_prompt, x_sample, cache_k, cache_v, cache_logf, state_C, state_n, state_m, page_table, p_prompt, p_sample, a_w_in, a_b_gate, a_gn_g, a_w_out, kv_ln_g, kv_ln_b, kv_w, kv_b_f, b_w_q, b_w_o, ln_mix_g, ln_mix_b, ln_ffn_g, ln_ffn_b, peer_w_q, peer_subkeys, peer_u, peer_v, ple_w_p, ple_w_g, ple_b_g):
    bsz, seq, _ = x_prompt.shape
    db = x_sample.shape[0]
    n_pool, page = cache_k.shape[:2]
    sdt = state_C.dtype
    ldt = cache_logf.dtype

    w_in_bf = a_w_in[0][:, :4 * D].astype(BF16)
    w_gate = _pad_cols(a_w_in[0][:, 4 * D:])
    b_gate = _pad_cols(a_b_gate[0].reshape(1, -1))
    w_out_bf = a_w_out[0].astype(BF16)
    wk_bf = kv_w[:, :D].astype(BF16)
    wv_bf = kv_w[:, D:2 * D].astype(BF16)
    wf = _pad_cols(kv_w[:, 2 * D:])
    bf = _pad_cols(kv_b_f.reshape(1, -1))
    bwq_bf = b_w_q[0].astype(BF16)
    bwo_bf = b_w_o[0].astype(BF16)
    pwq_bf = peer_w_q.astype(BF16)
    sk_bf = peer_subkeys.astype(BF16)
    u_tabs = [_pack_table(peer_u[i]) for i in range(DEPTH)]
    v_tabs = [_pack_table(peer_v[i]) for i in range(DEPTH)]
    wg_bf = ple_w_g.astype(BF16)
    wp_bf = ple_w_p.astype(BF16)

    def channel_mix(i, x_mid, p):
        po = peer(x_mid, pwq_bf[i], sk_bf[i], u_tabs[i], v_tabs[i])
        return ln_ple(x_mid, po, ln_ffn_g[i], ln_ffn_b[i], p, wg_bf[i], ple_b_g[i], wp_bf[i])

    def shared_kv(x):
        k, v, lfp = ln_kv(x, kv_ln_g, kv_ln_b, wk_bf, wv_bf, wf, bf)
        return k, v, lfp[:, :HEADS]

    t = bsz * seq
    x = x_prompt.reshape(t, D)
    p = p_prompt.reshape(DEPTH, t, -1)
    qkvo = matmul(x, w_in_bf)
    gates = gates_matmul(x, w_gate, b_gate, HEADS)
    y_pre, c_p, n_p, m_p = mlstm_prompt(qkvo, gates, a_gn_g[0], bsz, seq)
    x = matmul_ln(y_pre, w_out_bf, x, ln_mix_g[0], ln_mix_b[0])
    x = channel_mix(0, x, p[0])

    k_p, v_p, lf_p = shared_kv(x)
    lf_t = lf_p.reshape(bsz, seq, HEADS).transpose(0, 2, 1)
    suf_t = forget_suffix(lf_t)
    suf_tok = suf_t.transpose(0, 2, 1).reshape(t, HEADS)
    q = matmul(x, bwq_bf)
    o = fox_prompt(q, k_p, v_p, suf_tok, suf_t, bsz, seq)
    x = matmul_ln(o, bwo_bf, x, ln_mix_g[1], ln_mix_b[1])
    y_prompt = channel_mix(1, x, p[1]).reshape(bsz, seq, D)

    xs = _pad_rows(x_sample.reshape(db, D), DEC_PAD)
    ps = jnp.pad(p_sample.reshape(DEPTH, db, -1), ((0, 0), (0, DEC_PAD - db), (0, 0)))
    qkvo_s = matmul(xs, w_in_bf)
    gates_s = gates_matmul(xs, w_gate, b_gate, HEADS)
    y_s, c_s, n_s, m_s = mlstm_step(qkvo_s[:db].reshape(db, 1, 4 * D), gates_s[:db].reshape(db, 1, LANES),
                                    a_gn_g[0], state_C[0].astype(F32), state_n[0].astype(F32),
                                    state_m[0].astype(F32))
    xs = matmul_ln(_pad_rows(y_s.reshape(db, D), DEC_PAD), w_out_bf, xs, ln_mix_g[0], ln_mix_b[0])
    xs = channel_mix(0, xs, ps[0])

    k_s, v_s, lf_s = shared_kv(xs)
    q_s = matmul(xs, bwq_bf)
    o_s = fox_decode(q_s[:db].reshape(db, HEADS, HD), k_s[:db].reshape(db, HEADS, HD), v_s[:db].reshape(db, HEADS, HD),
                     lf_s[:db], cache_k, cache_v, cache_logf.astype(F32), page_table)
    xs = matmul_ln(_pad_rows(o_s.reshape(db, D), DEC_PAD), bwo_bf, xs, ln_mix_g[1], ln_mix_b[1])
    y_sample = channel_mix(1, xs, ps[1])[:db].reshape(db, 1, D)

    return (y_prompt, y_sample,
            k_p.reshape(bsz, seq, HEADS, HD), v_p.reshape(bsz, seq, HEADS, HD),
            lf_p.reshape(bsz, seq, HEADS).astype(ldt),
            c_p[None].astype(sdt), n_p[None].astype(sdt), m_p[:, :, 0][None].astype(sdt),
            k_s[:db].reshape(db, 1, HEADS, HD), v_s[:db].reshape(db, 1, HEADS, HD),
            lf_s[:db].reshape(db, 1, HEADS).astype(ldt),
            c_s[None].astype(sdt), n_s[None].astype(sdt), m_s[:, 0, :HEADS][None].astype(sdt))
```

```python
import functools
import math

import jax
import jax.numpy as jnp
from jax import lax
from jax.experimental import pallas as pl
from jax.experimental.pallas import tpu as pltpu

F32 = jnp.float32
BF16 = jnp.bfloat16
I32 = jnp.int32

D = 1024
HEADS = 8
HD = 128
DEPTH = 2
PEER_HEADS = 8
PEER_NKEYS = 128
PEER_TOPK = 16
ALPHA = (2.0 * DEPTH) ** 0.25
LN_EPS = 1e-5
HEAD_NORM_EPS = 1e-6
CHUNK = 128
DEC_PAD = 128

LANES = 128
VMEM_LIMIT = 56 * 1024 * 1024

NEG_INF = float("-inf")


def _cparams(sem, vmem=None):
    return pltpu.CompilerParams(dimension_semantics=sem, vmem_limit_bytes=vmem or VMEM_LIMIT)


def _layer_norm(z, g, b):
    mu = jnp.mean(z, axis=-1, keepdims=True)
    zc = z - mu
    var = jnp.mean(zc * zc, axis=-1, keepdims=True)
    return zc * lax.rsqrt(var + LN_EPS) * g + b


def _log_sigmoid(z):
    return jnp.minimum(z, 0.0) - jnp.log1p(jnp.exp(-jnp.abs(z)))


def _split_bf16(a):
    hi = a.astype(BF16)
    lo = (a - hi.astype(F32)).astype(BF16)
    return hi, lo


def _dot(a, b):
    return jnp.dot(a, b, preferred_element_type=F32)


def _dot_nt(a, b):
    return lax.dot_general(a, b, (((1,), (1,)), ((), ())), preferred_element_type=F32)


def _dot3(x, w):
    xh, xl = _split_bf16(x)
    wh, wl = _split_bf16(w)
    return _dot(xh, wh) + _dot(xl, wh) + _dot(xh, wl)


def _mm_kernel(x_ref, w_ref, o_ref, xb_ref):
    @pl.when(pl.program_id(1) == 0)
    def _():
        xb_ref[...] = x_ref[...].astype(BF16)

    o_ref[...] = _dot(xb_ref[...], w_ref[...])


def matmul(x, w_bf, tm=1024, tn=1024):
    m, k = x.shape
    n = w_bf.shape[1]
    tm = min(tm, m)
    tn = min(tn, n)
    return pl.pallas_call(
        _mm_kernel,
        grid=(m // tm, n // tn),
        in_specs=[pl.BlockSpec((tm, k), lambda i, j: (i, 0)),
                  pl.BlockSpec((k, tn), lambda i, j: (0, j))],
        out_specs=pl.BlockSpec((tm, tn), lambda i, j: (i, j)),
        out_shape=jax.ShapeDtypeStruct((m, n), F32),
        scratch_shapes=[pltpu.VMEM((tm, k), BF16)],
        compiler_params=_cparams(("parallel", "arbitrary")),
        name="matmul",
    )(x, w_bf)


def _gates_kernel(x_ref, w_ref, b_ref, o_ref, *, ls_start):
    z = _dot3(x_ref[...], w_ref[...]) + b_ref[...]
    col = lax.broadcasted_iota(I32, z.shape, 1)
    o_ref[...] = jnp.where(col >= ls_start, _log_sigmoid(z), z)


def gates_matmul(x, w_pad, b_pad, ls_start, tm=512):
    m, k = x.shape
    tm = min(tm, m)
    return pl.pallas_call(
        functools.partial(_gates_kernel, ls_start=ls_start),
        grid=(m // tm,),
        in_specs=[pl.BlockSpec((tm, k), lambda i: (i, 0)),
                  pl.BlockSpec((k, LANES), lambda i: (0, 0)),
                  pl.BlockSpec((1, LANES), lambda i: (0, 0))],
        out_specs=pl.BlockSpec((tm, LANES), lambda i: (i, 0)),
        out_shape=jax.ShapeDtypeStruct((m, LANES), F32),
        compiler_params=_cparams(("parallel",)),
        name="gates_matmul",
    )(x, w_pad, b_pad)


def _mm_ln_kernel(a_ref, w_ref, r_ref, g_ref, b_ref, o_ref):
    y = _dot(a_ref[...].astype(BF16), w_ref[...])
    o_ref[...] = _layer_norm(ALPHA * r_ref[...] + y, g_ref[...], b_ref[...])


def matmul_ln(a, w_bf, res, g, b, tm=512):
    m, k = a.shape
    n = w_bf.shape[1]
    tm = min(tm, m)
    return pl.pallas_call(
        _mm_ln_kernel,
        grid=(m // tm,),
        in_specs=[pl.BlockSpec((tm, k), lambda i: (i, 0)),
                  pl.BlockSpec((k, n), lambda i: (0, 0)),
                  pl.BlockSpec((tm, n), lambda i: (i, 0)),
                  pl.BlockSpec((1, n), lambda i: (0, 0)),
                  pl.BlockSpec((1, n), lambda i: (0, 0))],
        out_specs=pl.BlockSpec((tm, n), lambda i: (i, 0)),
        out_shape=jax.ShapeDtypeStruct((m, n), F32),
        compiler_params=_cparams(("parallel",)),
        name="matmul_ln",
    )(a, w_bf, res, g.reshape(1, n), b.reshape(1, n))


def _ln_ple_kernel(xm_ref, po_ref, g_ref, b_ref, p_ref, wg_ref, bg_ref, wp_ref, o_ref):
    x2 = _layer_norm(ALPHA * xm_ref[...] + po_ref[...], g_ref[...], b_ref[...])
    gate = jax.nn.sigmoid(_dot(x2.astype(BF16), wg_ref[...]) + bg_ref[...])
    emb = _dot(p_ref[...].astype(BF16), wp_ref[...])
    o_ref[...] = x2 + gate * emb


def ln_ple(xm, po, g, b, p, wg_bf, bg, wp_bf, tm=512):
    m, n = xm.shape
    kp = p.shape[1]
    tm = min(tm, m)
    row = lambda i: (i, 0)
    fixed = lambda i: (0, 0)
    return pl.pallas_call(
        _ln_ple_kernel,
        grid=(m // tm,),
        in_specs=[pl.BlockSpec((tm, n), row), pl.BlockSpec((tm, n), row),
                  pl.BlockSpec((1, n), fixed), pl.BlockSpec((1, n), fixed),
                  pl.BlockSpec((tm, kp), row),
                  pl.BlockSpec((n, n), fixed), pl.BlockSpec((1, n), fixed),
                  pl.BlockSpec((kp, n), fixed)],
        out_specs=pl.BlockSpec((tm, n), row),
        out_shape=jax.ShapeDtypeStruct((m, n), F32),
        compiler_params=_cparams(("parallel",)),
        name="ln_ple",
    )(xm, po, g.reshape(1, n), b.reshape(1, n), p, wg_bf, bg.reshape(1, n), wp_bf)


def _ln_kv_kernel(x_ref, g_ref, b_ref, wk_ref, wv_ref, wf_ref, bf_ref, k_ref, v_ref, lf_ref):
    st = _layer_norm(x_ref[...], g_ref[...], b_ref[...])
    sb = st.astype(BF16)
    k_ref[...] = _dot(sb, wk_ref[...])
    v_ref[...] = _dot(sb, wv_ref[...])
    lf_ref[...] = _log_sigmoid(_dot3(st, wf_ref[...]) + bf_ref[...])


def ln_kv(x, g, b, wk_bf, wv_bf, wf_pad, bf_pad, tm=512):
    m, n = x.shape
    tm = min(tm, m)
    row = lambda i: (i, 0)
    fixed = lambda i: (0, 0)
    return pl.pallas_call(
        _ln_kv_kernel,
        grid=(m // tm,),
        in_specs=[pl.BlockSpec((tm, n), row), pl.BlockSpec((1, n), fixed), pl.BlockSpec((1, n), fixed),
                  pl.BlockSpec((n, n), fixed), pl.BlockSpec((n, n), fixed),
                  pl.BlockSpec((n, LANES), fixed), pl.BlockSpec((1, LANES), fixed)],
        out_specs=[pl.BlockSpec((tm, n), row), pl.BlockSpec((tm, n), row), pl.BlockSpec((tm, LANES), row)],
        out_shape=[jax.ShapeDtypeStruct((m, n), F32), jax.ShapeDtypeStruct((m, n), F32),
                   jax.ShapeDtypeStruct((m, LANES), F32)],
        compiler_params=_cparams(("parallel",)),
        name="ln_kv",
    )(x, g.reshape(1, n), b.reshape(1, n), wk_bf, wv_bf, wf_pad, bf_pad)


def _head_norm_gate(hh, o_pre, gn):
    mu = jnp.mean(hh, axis=-1, keepdims=True)
    hc = hh - mu
    var = jnp.mean(hc * hc, axis=-1, keepdims=True)
    return jax.nn.sigmoid(o_pre) * (hc * lax.rsqrt(var + HEAD_NORM_EPS)) * gn


def _mlstm_prompt_kernel(q_ref, k_ref, v_ref, o_ref, g_ref, gn_ref,
                         y_ref, c_out_ref, n_out_ref, m_out_ref, c_ref, m_ref):
    c_idx = pl.program_id(1)
    L = CHUNK

    @pl.when(c_idx == 0)
    def _():
        c_ref[...] = jnp.zeros_like(c_ref)
        m_ref[...] = jnp.zeros_like(m_ref)

    row = lax.broadcasted_iota(I32, (L, L), 0)
    col = lax.broadcasted_iota(I32, (L, L), 1)
    causal = col <= row
    tril = causal.astype(F32)
    g = g_ref[...]
    cum = jnp.dot(tril, g, preferred_element_type=F32, precision=lax.Precision.HIGHEST)
    g_t = g.T
    cum_t = cum.T
    lane = lax.broadcasted_iota(I32, (L, HD), 1)
    ones_col = (lane == 0).astype(BF16)
    scale = HD ** -0.5

    heads = range(HEADS)
    sls = [slice(h * HD, (h + 1) * HD) for h in heads]
    qbs = [q_ref[:, sl].astype(BF16) for sl in sls]
    kbs = [(k_ref[:, sl] * scale).astype(BF16) for sl in sls]
    c_augs = [c_ref[h] for h in heads]
    qks = [_dot_nt(qbs[h], kbs[h]) for h in heads]
    a_inters = [_dot(qbs[h], c_augs[h].astype(BF16)) for h in heads]

    pre = []
    for h in heads:
        li_c = g[:, h:h + 1]
        b_c = cum[:, HEADS + h:HEADS + h + 1]
        m_prev = m_ref[h:h + 1, 0:1]
        dmat = jnp.where(causal, b_c - cum_t[HEADS + h:HEADS + h + 1, :] + g_t[h:h + 1, :], NEG_INF)
        b_last = b_c[L - 1:L, :]
        g_c = b_last - b_c + li_c
        pre.append((dmat, jnp.max(dmat, axis=1, keepdims=True), b_c + m_prev, g_c,
                    jnp.max(g_c, axis=0, keepdims=True), b_last + m_prev))

    stats, wide = [], []
    for h in heads:
        dmat, dmax, inter, g_c, gmax, carry = pre[h]
        m_t = jnp.maximum(inter, dmax)
        w_inter = jnp.exp(inter - m_t)
        m_new = jnp.maximum(carry, gmax)
        decay = jnp.exp(carry - m_new)
        w_c = jnp.exp(g_c - m_new)
        stats.append((m_t, w_inter, m_new, decay))
        wide.append((jnp.broadcast_to(m_t, (L, L)), jnp.broadcast_to(w_c, (L, HD))))

    ps, vws = [], []
    for h in heads:
        m_t_wide, w_c_wide = wide[h]
        ps.append((qks[h] * jnp.exp(pre[h][0] - m_t_wide)).astype(BF16))
        vws.append(jnp.concatenate([v_ref[:, sls[h]] * w_c_wide, jnp.where(lane == 0, w_c_wide, 0.0)],
                                   axis=1).astype(BF16))

    a_intras = [_dot(ps[h], jnp.concatenate([v_ref[:, sls[h]].astype(BF16), ones_col], axis=1)) for h in heads]
    upds = [lax.dot_general(kbs[h], vws[h], (((0,), (0,)), ((), ())), preferred_element_type=F32) for h in heads]

    hhs, mus = [], []
    for h in heads:
        m_t, w_inter, m_new, decay = stats[h]
        a_inter, a_intra = a_inters[h], a_intras[h]
        num = w_inter * a_inter[:, :HD] + a_intra[:, :HD]
        den = w_inter * a_inter[:, HD:HD + 1] + a_intra[:, HD:HD + 1]
        hh = num / jnp.maximum(jnp.abs(den), jnp.exp(-m_t))
        hhs.append(hh)
        mus.append(jnp.mean(hh, axis=-1, keepdims=True))
        c_ref[h] = decay * c_augs[h] + upds[h]
        m_ref[h:h + 1, :] = jnp.broadcast_to(m_new, (1, LANES))
    hcs = [hhs[h] - mus[h] for h in heads]
    variances = [jnp.mean(hc * hc, axis=-1, keepdims=True) for hc in hcs]
    for h in heads:
        hn = hcs[h] * lax.rsqrt(variances[h] + HEAD_NORM_EPS)
        y_ref[:, sls[h]] = jax.nn.sigmoid(o_ref[:, sls[h]]) * hn * gn_ref[:, sls[h]]

    @pl.when(c_idx == pl.num_programs(1) - 1)
    def _():
        for h in range(HEADS):
            c_aug = c_ref[h]
            c_out_ref[0, h] = c_aug[:, :HD]
            n_out_ref[0, h:h + 1, :] = c_aug[:, HD:].T[0:1, :]
        m_out_ref[0] = m_ref[...]


def mlstm_prompt(qkvo, gates, gn_g, bsz, seq):
    nc = seq // CHUNK
    t = bsz * seq

    def col_spec(gidx):
        return pl.BlockSpec((CHUNK, D), lambda b, c: (b * nc + c, gidx))

    return pl.pallas_call(
        _mlstm_prompt_kernel,
        grid=(bsz, nc),
        in_specs=[col_spec(0), col_spec(1), col_spec(2), col_spec(3),
                  pl.BlockSpec((CHUNK, LANES), lambda b, c: (b * nc + c, 0)),
                  pl.BlockSpec((1, D), lambda b, c: (0, 0))],
        out_specs=[pl.BlockSpec((CHUNK, D), lambda b, c: (b * nc + c, 0)),
                   pl.BlockSpec((1, HEADS, HD, HD), lambda b, c: (b, 0, 0, 0)),
                   pl.BlockSpec((1, HEADS, HD), lambda b, c: (b, 0, 0)),
                   pl.BlockSpec((1, HEADS, LANES), lambda b, c: (b, 0, 0))],
        out_shape=[jax.ShapeDtypeStruct((t, D), F32),
                   jax.ShapeDtypeStruct((bsz, HEADS, HD, HD), F32),
                   jax.ShapeDtypeStruct((bsz, HEADS, HD), F32),
                   jax.ShapeDtypeStruct((bsz, HEADS, LANES), F32)],
        scratch_shapes=[pltpu.VMEM((HEADS, HD, 2 * HD), F32), pltpu.VMEM((HEADS, LANES), F32)],
        compiler_params=_cparams(("parallel", "arbitrary")),
        name="mlstm_prompt",
    )(qkvo, qkvo, qkvo, qkvo, gates, gn_g.reshape(1, D))


def _mlstm_step_kernel(q_ref, k_ref, v_ref, o_ref, g_ref, gn_ref, c0_ref, n0_ref, m0_ref,
                       y_ref, c_out_ref, n_out_ref, m_out_ref):
    row = lax.broadcasted_iota(I32, (HD, HD), 0)
    col = lax.broadcasted_iota(I32, (HD, HD), 1)
    eye = row == col
    scale = HD ** -0.5
    g = g_ref[0]
    m0 = m0_ref[0]
    m_new_all = jnp.zeros((1, LANES), F32)
    lane = lax.broadcasted_iota(I32, (1, LANES), 1)

    def to_col(r):
        return jnp.sum(jnp.where(eye, r, 0.0), axis=1, keepdims=True)

    for h in range(HEADS):
        sl = slice(h * HD, (h + 1) * HD)
        q = q_ref[0][:, sl]
        ks = k_ref[0][:, sl] * scale
        v = v_ref[0][:, sl]
        li = g[:, h:h + 1]
        lf = g[:, HEADS + h:HEADS + h + 1]
        m_prev = m0[:, h:h + 1]
        c0 = c0_ref[0, h]
        n0 = n0_ref[0, h:h + 1, :]
        inter = lf + m_prev
        m_t = jnp.maximum(inter, li)
        w_inter = jnp.exp(inter - m_t)
        p = jnp.exp(li - m_t)
        qk = jnp.sum(q * ks, axis=1, keepdims=True) * p
        q_c = jnp.sum(to_col(q) * c0, axis=0, keepdims=True)
        q_n = jnp.sum(q * n0, axis=1, keepdims=True)
        num = w_inter * q_c + qk * v
        den = w_inter * q_n + qk
        hh = num / jnp.maximum(jnp.abs(den), jnp.exp(-m_t))
        y_ref[0, :, sl] = _head_norm_gate(hh, o_ref[0][:, sl], gn_ref[:, sl])
        decay = w_inter
        c_out_ref[0, h] = decay * c0 + to_col(p * ks) * v
        n_out_ref[0, h:h + 1, :] = decay * n0 + p * ks
        m_new_all = jnp.where(lane == h, m_t, m_new_all)
    m_out_ref[0] = m_new_all


def mlstm_step(qkvo3, gates3, gn_g, c0, n0, m0):
    nb = c0.shape[0]

    def col_spec(gidx):
        return pl.BlockSpec((1, 1, D), lambda b: (b, 0, gidx))

    return pl.pallas_call(
        _mlstm_step_kernel,
        grid=(nb,),
        in_specs=[col_spec(0), col_spec(1), col_spec(2), col_spec(3),
                  pl.BlockSpec((1, 1, LANES), lambda b: (b, 0, 0)),
                  pl.BlockSpec((1, D), lambda b: (0, 0)),
                  pl.BlockSpec((1, HEADS, HD, HD), lambda b: (b, 0, 0, 0)),
                  pl.BlockSpec((1, HEADS, HD), lambda b: (b, 0, 0)),
                  pl.BlockSpec((1, 1, HEADS), lambda b: (b, 0, 0))],
        out_specs=[pl.BlockSpec((1, 1, D), lambda b: (b, 0, 0)),
                   pl.BlockSpec((1, HEADS, HD, HD), lambda b: (b, 0, 0, 0)),
                   pl.BlockSpec((1, HEADS, HD), lambda b: (b, 0, 0)),
                   pl.BlockSpec((1, 1, LANES), lambda b: (b, 0, 0))],
        out_shape=[jax.ShapeDtypeStruct((nb, 1, D), F32),
                   jax.ShapeDtypeStruct((nb, HEADS, HD, HD), F32),
                   jax.ShapeDtypeStruct((nb, HEADS, HD), F32),
                   jax.ShapeDtypeStruct((nb, 1, LANES), F32)],
        compiler_params=_cparams(("parallel",)),
        name="mlstm_step",
    )(qkvo3, qkvo3, qkvo3, qkvo3, gates3, gn_g.reshape(1, D), c0, n0, m0.reshape(nb, 1, HEADS))


def _suffix_kernel(lf_ref, o_ref):
    s = lf_ref.shape[2]
    row = lax.broadcasted_iota(I32, (LANES, LANES), 0)
    col = lax.broadcasted_iota(I32, (LANES, LANES), 1)
    upper = (row > col).astype(F32)
    carry = jnp.zeros((HEADS, 1), F32)
    for c in reversed(range(s // LANES)):
        x = lf_ref[0, :, c * LANES:(c + 1) * LANES]
        inner = jnp.dot(x, upper, preferred_element_type=F32, precision=lax.Precision.HIGHEST)
        o_ref[0, :, c * LANES:(c + 1) * LANES] = inner + carry
        carry = carry + jnp.sum(x, axis=1, keepdims=True)


def forget_suffix(lf_t):
    bsz, nh, s = lf_t.shape
    return pl.pallas_call(
        _suffix_kernel,
        grid=(bsz,),
        in_specs=[pl.BlockSpec((1, nh, s), lambda b: (b, 0, 0))],
        out_specs=pl.BlockSpec((1, nh, s), lambda b: (b, 0, 0)),
        out_shape=jax.ShapeDtypeStruct((bsz, nh, s), F32),
        compiler_params=_cparams(("parallel",)),
        name="forget_suffix",
    )(lf_t)


def _fox_prompt_kernel(q_ref, k_ref, v_ref, sq_ref, sk_ref, o_ref, acc_ref, m_ref, l_ref, *, tq):
    i = pl.program_id(1)
    j = pl.program_id(2)
    scale = HD ** -0.5

    @pl.when(j == 0)
    def _():
        acc_ref[...] = jnp.zeros_like(acc_ref)
        m_ref[...] = jnp.full_like(m_ref, NEG_INF)
        l_ref[...] = jnp.zeros_like(l_ref)

    def update(on_diagonal):
        sq = sq_ref[...]
        sk = sk_ref[0]
        if on_diagonal:
            visible = lax.broadcasted_iota(I32, (tq, tq), 1) <= lax.broadcasted_iota(I32, (tq, tq), 0)
        for h in range(HEADS):
            sl = slice(h * HD, (h + 1) * HD)
            sqb = jnp.broadcast_to(sq[:, h:h + 1], (tq, LANES))
            s = _dot_nt(q_ref[:, sl].astype(BF16), k_ref[:, sl].astype(BF16)) * scale + sk[h:h + 1, :]
            if on_diagonal:
                s = jnp.where(visible, s, NEG_INF)
            m_prev = m_ref[h]
            m_new = jnp.maximum(m_prev, jnp.max(s, axis=1, keepdims=True) - sqb)
            alpha = jnp.exp(m_prev - m_new)
            p = jnp.exp(s - jnp.concatenate([m_new + sqb] * (tq // LANES), axis=1))
            l_ref[h] = alpha * l_ref[h] + jnp.sum(p, axis=1, keepdims=True)
            acc_ref[:, sl] = alpha * acc_ref[:, sl] + _dot(p.astype(BF16), v_ref[:, sl].astype(BF16))
            m_ref[h] = m_new

    @pl.when(j < i)
    def _():
        update(False)

    @pl.when(j == i)
    def _():
        update(True)
        for h in range(HEADS):
            sl = slice(h * HD, (h + 1) * HD)
            o_ref[:, sl] = acc_ref[:, sl] / l_ref[h]


def fox_prompt(q, k, v, suf_tok, suf_t, bsz, seq, tq=512):
    tq = min(tq, seq)
    nq = seq // tq
    t = bsz * seq
    kv_spec = pl.BlockSpec((tq, D), lambda b, i, j: (b * nq + jnp.minimum(j, i), 0))
    return pl.pallas_call(
        functools.partial(_fox_prompt_kernel, tq=tq),
        grid=(bsz, nq, nq),
        in_specs=[pl.BlockSpec((tq, D), lambda b, i, j: (b * nq + i, 0)),
                  kv_spec, kv_spec,
                  pl.BlockSpec((tq, HEADS), lambda b, i, j: (b * nq + i, 0)),
                  pl.BlockSpec((1, HEADS, tq), lambda b, i, j: (b, 0, jnp.minimum(j, i)))],
        out_specs=pl.BlockSpec((tq, D), lambda b, i, j: (b * nq + i, 0)),
        out_shape=jax.ShapeDtypeStruct((t, D), F32),
        scratch_shapes=[pltpu.VMEM((tq, D), F32), pltpu.VMEM((HEADS, tq, LANES), F32),
                        pltpu.VMEM((HEADS, tq, LANES), F32)],
        compiler_params=_cparams(("parallel", "parallel", "arbitrary")),
        name="fox_prompt",
    )(q, k, v, suf_tok, suf_t)


def _page_suffix_kernel(lf_ref, suf_ref, tot_ref):
    w = lf_ref.shape[1]
    row = lax.broadcasted_iota(I32, (w, w), 0)
    col = lax.broadcasted_iota(I32, (w, w), 1)
    same_head = (row % HEADS) == (col % HEADS)
    later = (same_head & (row > col)).astype(BF16)
    whole = same_head.astype(BF16)
    x = lf_ref[...]
    hi = x.astype(BF16)
    r1 = x - hi.astype(F32)
    mid = r1.astype(BF16)
    lo = (r1 - mid.astype(F32)).astype(BF16)
    suf_ref[...] = _dot(hi, later) + _dot(mid, later) + _dot(lo, later)
    tot_ref[...] = _dot(hi, whole) + _dot(mid, whole) + _dot(lo, whole)


def page_suffix(lf_flat, tr=512):
    r, w = lf_flat.shape
    tr = math.gcd(r, tr)
    spec = pl.BlockSpec((tr, w), lambda i: (i, 0))
    return pl.pallas_call(
        _page_suffix_kernel,
        grid=(r // tr,),
        in_specs=[spec],
        out_specs=[spec, spec],
        out_shape=[jax.ShapeDtypeStruct((r, w), F32), jax.ShapeDtypeStruct((r, w), F32)],
        compiler_params=_cparams(("parallel",)),
        name="page_suffix",
    )(lf_flat)


def _fox_decode_kernel(pt_ref, q_ref, kn_ref, vn_ref, lfn_ref, *refs, pg):
    page_refs = refs[:4 * pg]
    o_ref, acc_ref, m_ref, l_ref, r_ref = refs[4 * pg:]
    j = pl.program_id(1)
    scale = HD ** -0.5
    w = r_ref.shape[1]
    q8 = q_ref[0]

    @pl.when(j == 0)
    def _():
        s_new = jnp.sum(q8 * kn_ref[0], axis=1, keepdims=True) * scale
        m_ref[...] = jnp.broadcast_to(s_new, m_ref.shape)
        l_ref[...] = jnp.ones_like(l_ref)
        acc_ref[...] = vn_ref[0]
        r_ref[...] = lfn_ref[0]

    run = r_ref[...]
    bias, ks, vs = [], [], []
    for i in range(pg):
        k_ref, v_ref, suf_ref, tot_ref = page_refs[4 * i:4 * i + 4]
        bias.append(suf_ref[0] + run)
        run = run + tot_ref[0]
        ks.append(k_ref[0].reshape(w, HD).astype(BF16))
        vs.append(v_ref[0].reshape(w, HD).astype(BF16))
    r_ref[...] = run
    own = (lax.broadcasted_iota(I32, (HEADS, pg * w), 1) % HEADS) == lax.broadcasted_iota(I32, (HEADS, pg * w), 0)
    s = _dot_nt(q8.astype(BF16), jnp.concatenate(ks, axis=0)) * scale + jnp.concatenate(bias, axis=1)
    s = jnp.where(own, s, NEG_INF)
    m_prev = m_ref[:, 0:1]
    m_new = jnp.maximum(m_prev, jnp.max(s, axis=1, keepdims=True))
    alpha = jnp.exp(m_prev - m_new)
    p = jnp.exp(s - m_new)
    l_ref[...] = alpha * l_ref[...] + jnp.sum(p, axis=1, keepdims=True)
    acc_ref[...] = alpha * acc_ref[...] + _dot(p.astype(BF16), jnp.concatenate(vs, axis=0))
    m_ref[...] = jnp.broadcast_to(m_new, m_ref.shape)

    @pl.when(j == pl.num_programs(1) - 1)
    def _():
        o_ref[0] = acc_ref[...] / l_ref[...]


def fox_decode(q, k, v, lf_new, cache_k, cache_v, cache_logf, page_table, pages_per_step=8):
    nb, n_pages = page_table.shape
    n_pool, page = cache_k.shape[:2]
    w = page * HEADS
    pg = math.gcd(n_pages, pages_per_step)
    suf_in, tot = page_suffix(cache_logf.reshape(n_pool, w))
    suf_in = suf_in.reshape(n_pool, 1, w)
    tot = tot.reshape(n_pool, 1, w)
    lfn = jnp.tile(lf_new, (1, page)).reshape(nb, 1, w)
    tok = lambda b, j, pt: (b, 0, 0)
    in_specs = [pl.BlockSpec((1, HEADS, HD), tok), pl.BlockSpec((1, HEADS, HD), tok), pl.BlockSpec((1, HEADS, HD), tok),
                pl.BlockSpec((1, 1, w), tok)]
    operands = []
    for i in range(pg):
        sel4 = lambda b, j, pt, i=i: (pt[b, n_pages - 1 - (j * pg + i)], 0, 0, 0)
        sel3 = lambda b, j, pt, i=i: (pt[b, n_pages - 1 - (j * pg + i)], 0, 0)
        in_specs += [pl.BlockSpec((1, page, HEADS, HD), sel4), pl.BlockSpec((1, page, HEADS, HD), sel4),
                     pl.BlockSpec((1, 1, w), sel3), pl.BlockSpec((1, 1, w), sel3)]
        operands += [cache_k, cache_v, suf_in, tot]
    grid_spec = pltpu.PrefetchScalarGridSpec(
        num_scalar_prefetch=1,
        grid=(nb, n_pages // pg),
        in_specs=in_specs,
        out_specs=pl.BlockSpec((1, HEADS, HD), tok),
        scratch_shapes=[pltpu.VMEM((HEADS, HD), F32), pltpu.VMEM((HEADS, LANES), F32),
                        pltpu.VMEM((HEADS, LANES), F32), pltpu.VMEM((1, w), F32)],
    )
    return pl.pallas_call(
        functools.partial(_fox_decode_kernel, pg=pg),
        grid_spec=grid_spec,
        out_shape=jax.ShapeDtypeStruct((nb, HEADS, HD), F32),
        compiler_params=_cparams(("parallel", "arbitrary")),
        name="fox_decode",
    )(page_table, q, k, v, lfn, *operands)


PICKS = PEER_HEADS * PEER_TOPK
EXPERT_BITS = 14
PAIR_ROWS = 8
TILE_ROWS = 2 * PAIR_ROWS
GROWS = PICKS * TILE_ROWS


def _take_topk(s, tie, k, put):
    big = 3.0e38
    for r in range(k):
        m = jnp.max(s, axis=0, keepdims=True)
        pos = jnp.min(jnp.where(s == m, tie, big), axis=0, keepdims=True)
        put(r, m, pos)
        s = jnp.where(tie == pos, NEG_INF, s)


def _take_topk_keys(s, k, put):
    g = s.shape[0] // 8
    base = lax.broadcasted_iota(I32, (8, s.shape[1]), 0).astype(F32)
    vals = [s[8 * v:8 * v + 8, :] for v in range(g)]
    ids = [base + float(8 * v) for v in range(g)]
    for rnd in range(g):
        for i in range(rnd % 2, g - 1, 2):
            swap = vals[i + 1] > vals[i]
            vals[i], vals[i + 1] = jnp.where(swap, vals[i + 1], vals[i]), jnp.where(swap, vals[i], vals[i + 1])
            ids[i], ids[i + 1] = jnp.where(swap, ids[i + 1], ids[i]), jnp.where(swap, ids[i], ids[i + 1])
    big = 3.0e38
    for r in range(k):
        m = jnp.max(vals[0], axis=0, keepdims=True)
        pos = jnp.min(jnp.where(vals[0] == m, ids[0], big), axis=0, keepdims=True)
        put(r, m, pos)
        taken = ids[0] == pos
        last = min(g, k) - 1 - r
        for v in range(last):
            vals[v] = jnp.where(taken, vals[v + 1], vals[v])
            ids[v] = jnp.where(taken, ids[v + 1], ids[v])
        if last >= 0:
            vals[last] = jnp.where(taken, NEG_INF, vals[last])


def _candidate_bins(kk):
    full, small = [], []
    for a in range(kk):
        nb = kk // (a + 1)
        b0 = 0
        while nb - b0 >= 8:
            full.append([(a, b0, 0, 8)])
            b0 += 8
        if nb > b0:
            small.append((a, b0, nb - b0))
    bins = []
    for a, b0, rows in sorted(small, key=lambda piece: -piece[2]):
        for group in bins:
            used = group[-1][2] + group[-1][3]
            if used + rows <= 8:
                group.append((a, b0, used, rows))
                break
        else:
            bins.append([(a, b0, 0, rows)])
    return full + bins


def _peer_select_kernel(x_ref, wq_ref, sk_ref, idx_ref, off_ref, g_ref, sc_ref, sv_ref, si_ref, e_ref, gg_ref):
    tb = x_ref.shape[0]
    kk = PEER_TOPK
    nk = PEER_NKEYS
    lw = LANES
    q = _dot(x_ref[...].astype(BF16), wq_ref[...]).astype(BF16)
    sub8 = lax.broadcasted_iota(I32, (8, lw), 0)
    sub8f = sub8.astype(F32)
    bins = _candidate_bins(kk)

    for h in range(PEER_HEADS):
        for c in range(2):
            qc = q[:, (2 * h + c) * nk:(2 * h + c + 1) * nk]
            sc_ref[c] = _dot_nt(sk_ref[c], qc)
        for part in range(tb // lw):
            ls = slice(part * lw, (part + 1) * lw)
            for c in range(2):
                def put1(r, m, pos, c=c):
                    sv_ref[c, r:r + 1, ls] = m
                    si_ref[c, r:r + 1, ls] = pos

                _take_topk_keys(sc_ref[c, :, ls], kk, put1)

            cand, ckey = [], []
            for pieces in bins:
                va = ia = vb = ib = flat = None
                used = 0
                for a, b0, start, rows in pieces:
                    ra = jnp.broadcast_to(sv_ref[0, a:a + 1, ls], (8, lw))
                    ri = jnp.broadcast_to(si_ref[0, a:a + 1, ls], (8, lw))
                    rb = sv_ref[1, b0:b0 + 8, ls]
                    rj = si_ref[1, b0:b0 + 8, ls]
                    rf = sub8f + float(a * kk + b0 - start)
                    if start:
                        rb = pltpu.roll(rb, start, axis=0)
                        rj = pltpu.roll(rj, start, axis=0)
                        here = sub8 >= start
                        va, ia, vb = jnp.where(here, ra, va), jnp.where(here, ri, ia), jnp.where(here, rb, vb)
                        ib, flat = jnp.where(here, rj, ib), jnp.where(here, rf, flat)
                    else:
                        va, ia, vb, ib, flat = ra, ri, rb, rj, rf
                    used = start + rows
                c8 = va + vb
                cand.append(c8 if used == 8 else jnp.where(sub8 < used, c8, NEG_INF))
                ckey.append(flat * float(2 ** EXPERT_BITS) + (ia * float(nk) + ib))
            cand = jnp.concatenate(cand, axis=0)
            ckey = jnp.concatenate(ckey, axis=0)

            def put2(r, m, pos, h=h):
                gg_ref[h * kk + r:h * kk + r + 1, ls] = m
                e_ref[h * kk + r:h * kk + r + 1, ls] = pos

            _take_topk(cand, ckey, kk, put2)
            cv = gg_ref[h * kk:(h + 1) * kk, ls]
            ex = jnp.exp(cv - cv[0:1, :])
            gg_ref[h * kk:(h + 1) * kk, ls] = ex / jnp.sum(ex, axis=0, keepdims=True)

    expert = e_ref[...].T.astype(I32) & (2 ** EXPERT_BITS - 1)
    idx_ref[...] = expert
    off_ref[...] = (expert >> 1) * PAIR_ROWS
    g_ref[...] = gg_ref[...].T


def peer_select(x, wq_bf, sk_bf, tb=256):
    m, n = x.shape
    nq = wq_bf.shape[1]
    tb = min(tb, m)
    out_spec = pl.BlockSpec((tb, PICKS), lambda i: (i, 0))
    return pl.pallas_call(
        _peer_select_kernel,
        grid=(m // tb,),
        in_specs=[pl.BlockSpec((tb, n), lambda i: (i, 0)),
                  pl.BlockSpec((n, nq), lambda i: (0, 0)),
                  pl.BlockSpec((2, PEER_NKEYS, PEER_NKEYS), lambda i: (0, 0, 0))],
        out_specs=[out_spec, out_spec, out_spec],
        out_shape=[jax.ShapeDtypeStruct((m, PICKS), I32), jax.ShapeDtypeStruct((m, PICKS), I32),
                   jax.ShapeDtypeStruct((m, PICKS), F32)],
        scratch_shapes=[pltpu.VMEM((2, PEER_NKEYS, tb), F32),
                        pltpu.VMEM((2, PEER_TOPK, tb), F32), pltpu.VMEM((2, PEER_TOPK, tb), F32),
                        pltpu.VMEM((PICKS, tb), F32), pltpu.VMEM((PICKS, tb), F32)],
        compiler_params=_cparams(("parallel",)),
        name="peer_select",
    )(x, wq_bf, sk_bf)


def _diag_mask():
    sub = lax.broadcasted_iota(I32, (8, GROWS), 0)
    lane = lax.broadcasted_iota(I32, (8, GROWS), 1)
    return (lane % TILE_ROWS) // 2 == sub


def _gather_pairs(off_smem, tab_ref, t):
    tiles = []
    for p in range(PICKS):
        off = pl.multiple_of(off_smem[t, p], PAIR_ROWS)
        tiles.append(pltpu.bitcast(tab_ref[pl.ds(off, PAIR_ROWS), :], BF16))
    return jnp.concatenate(tiles, axis=0)


TOKEN_UNROLL = 16


def _peer_up_kernel(off_smem, x_ref, idxv_ref, gate_ref, tab_ref, ecomp_ref, w_ref, a_ref):
    tbu = x_ref.shape[0]
    mdiag = _diag_mask()

    def body(t, carry):
        g = _gather_pairs(off_smem, tab_ref, t)
        xr = x_ref[pl.ds(t, 1), :]
        x8 = jnp.concatenate([xr[:, s * LANES:(s + 1) * LANES] for s in range(D // LANES)], axis=0)
        res = _dot_nt(x8.astype(BF16), g)
        a_ref[pl.ds(t, 1), :] = jnp.sum(jnp.where(mdiag, res, 0.0), axis=0, keepdims=True)
        return carry

    lax.fori_loop(0, tbu, body, 0, unroll=TOKEN_UNROLL)
    hi, lo = _split_bf16(a_ref[...])
    both = _dot(hi, ecomp_ref[...]) + _dot(lo, ecomp_ref[...])
    odd = (idxv_ref[...] & 1) == 1
    a = jnp.where(odd, both[:, PICKS:], both[:, :PICKS])
    w_ref[...] = gate_ref[...] * (0.5 * a * (1.0 + lax.erf(a * (2.0 ** -0.5))))


def _peer_down_kernel(off_smem, w_ref, idxv_ref, tab_ref, eexp_ref, o_ref, wx_ref):
    tbu = w_ref.shape[0]
    mdiag = _diag_mask()
    w = w_ref[...]
    odd = (idxv_ref[...] & 1) == 1
    w01 = jnp.concatenate([jnp.where(odd, 0.0, w), jnp.where(odd, w, 0.0)], axis=1).astype(BF16)
    wx_ref[...] = _dot(w01, eexp_ref[...])

    def body(t, carry):
        g = _gather_pairs(off_smem, tab_ref, t)
        wm = jnp.where(mdiag, wx_ref[pl.ds(t, 1), :], 0.0).astype(BF16)
        o_ref[t] = _dot(wm, g)
        return carry

    lax.fori_loop(0, tbu, body, 0, unroll=TOKEN_UNROLL)


def _expand_matrix():
    r = lax.broadcasted_iota(I32, (2 * PICKS, GROWS), 0)
    k = lax.broadcasted_iota(I32, (2 * PICKS, GROWS), 1)
    return ((k // TILE_ROWS == r % PICKS) & (k % 2 == r // PICKS)).astype(BF16)


def _table_spec(tab):
    return pl.BlockSpec(tab.shape, lambda i: (0, 0), pipeline_mode=pl.Buffered(1))


def peer_up(off, idx, x, gate, tab, tbu=128):
    m = x.shape[0]
    tbu = min(tbu, m)
    ecomp = _expand_matrix().T
    return pl.pallas_call(
        _peer_up_kernel,
        grid=(m // tbu,),
        in_specs=[pl.BlockSpec((tbu, PICKS), lambda i: (i, 0), memory_space=pltpu.SMEM),
                  pl.BlockSpec((tbu, D), lambda i: (i, 0)),
                  pl.BlockSpec((tbu, PICKS), lambda i: (i, 0)),
                  pl.BlockSpec((tbu, PICKS), lambda i: (i, 0)),
                  _table_spec(tab),
                  pl.BlockSpec((GROWS, 2 * PICKS), lambda i: (0, 0))],
        out_specs=pl.BlockSpec((tbu, PICKS), lambda i: (i, 0)),
        out_shape=jax.ShapeDtypeStruct((m, PICKS), F32),
        scratch_shapes=[pltpu.VMEM((tbu, GROWS), F32)],
        compiler_params=_cparams(("arbitrary",)),
        name="peer_up",
    )(off, x, idx, gate, tab, ecomp)


def peer_down(off, idx, w, tab, tbu=128):
    m = w.shape[0]
    tbu = min(tbu, m)
    eexp = _expand_matrix()
    return pl.pallas_call(
        _peer_down_kernel,
        grid=(m // tbu,),
        in_specs=[pl.BlockSpec((tbu, PICKS), lambda i: (i, 0), memory_space=pltpu.SMEM),
                  pl.BlockSpec((tbu, PICKS), lambda i: (i, 0)),
                  pl.BlockSpec((tbu, PICKS), lambda i: (i, 0)),
                  _table_spec(tab),
                  pl.BlockSpec((2 * PICKS, GROWS), lambda i: (0, 0))],
        out_specs=pl.BlockSpec((tbu, 8, LANES), lambda i: (i, 0, 0)),
        out_shape=jax.ShapeDtypeStruct((m, 8, LANES), F32),
        scratch_shapes=[pltpu.VMEM((tbu, GROWS), F32)],
        compiler_params=_cparams(("arbitrary",)),
        name="peer_down",
    )(off, w, idx, tab, eexp).reshape(m, D)


def _pack_table(tab):
    e = tab.shape[0]
    bits = lax.bitcast_convert_type(tab.astype(BF16), jnp.uint16).astype(jnp.uint32).reshape(e // 2, 2, D)
    words = bits[:, 0, :] | (bits[:, 1, :] << 16)
    return words.reshape(e // 2 * PAIR_ROWS, LANES)


def peer(x, wq_bf, sk_bf, u_tab, v_tab):
    idx, off, gate = peer_select(x, wq_bf, sk_bf)
    w = peer_up(off, idx, x, gate, u_tab)
    return peer_down(off, idx, w, v_tab)


def _pad_cols(w, n=LANES):
    return jnp.pad(w, ((0, 0), (0, n - w.shape[1])))


def _pad_rows(a, rows):
    return jnp.pad(a, ((0, rows - a.shape[0]),) + ((0, 0),) * (a.ndim - 1))


def kernel(x_prompt, x_sample, cache_k, cache_v, cache_logf, state_C, state_n, state_m, page_table, p_prompt, p_sample, a_w_in, a_b_gate, a_gn_g, a_w_out, kv_ln_g, kv_ln_b, kv_w, kv_b_f, b_w_q, b_w_o, ln_mix_g, ln_mix_b, ln_ffn_g, ln_ffn_b, peer_w_q, peer_subkeys, peer_u, peer_v, ple_w_p, ple_w_g, ple_b_g):
    bsz, seq, _ = x_prompt.shape
    db = x_sample.shape[0]
    n_pool, page = cache_k.shape[:2]
    sdt = state_C.dtype
    ldt = cache_logf.dtype

    w_in_bf = a_w_in[0][:, :4 * D].astype(BF16)
    w_gate = _pad_cols(a_w_in[0][:, 4 * D:])
    b_gate = _pad_cols(a_b_gate[0].reshape(1, -1))
    w_out_bf = a_w_out[0].astype(BF16)
    wk_bf = kv_w[:, :D].astype(BF16)
    wv_bf = kv_w[:, D:2 * D].astype(BF16)
    wf = _pad_cols(kv_w[:, 2 * D:])
    bf = _pad_cols(kv_b_f.reshape(1, -1))
    bwq_bf = b_w_q[0].astype(BF16)
    bwo_bf = b_w_o[0].astype(BF16)
    pwq_bf = peer_w_q.astype(BF16)
    sk_bf = peer_subkeys.astype(BF16)
    u_tabs = [_pack_table(peer_u[i]) for i in range(DEPTH)]
    v_tabs = [_pack_table(peer_v[i]) for i in range(DEPTH)]
    wg_bf = ple_w_g.astype(BF16)
    wp_bf = ple_w_p.astype(BF16)

    def channel_mix(i, x_mid, p):
        po = peer(x_mid, pwq_bf[i], sk_bf[i], u_tabs[i], v_tabs[i])
        return ln_ple(x_mid, po, ln_ffn_g[i], ln_ffn_b[i], p, wg_bf[i], ple_b_g[i], wp_bf[i])

    def shared_kv(x):
        k, v, lfp = ln_kv(x, kv_ln_g, kv_ln_b, wk_bf, wv_bf, wf, bf)
        return k, v, lfp[:, :HEADS]

    t = bsz * seq
    x = x_prompt.reshape(t, D)
    p = p_prompt.reshape(DEPTH, t, -1)
    qkvo = matmul(x, w_in_bf)
    gates = gates_matmul(x, w_gate, b_gate, HEADS)
    y_pre, c_p, n_p, m_p = mlstm_prompt(qkvo, gates, a_gn_g[0], bsz, seq)
    x = matmul_ln(y_pre, w_out_bf, x, ln_mix_g[0], ln_mix_b[0])
    x = channel_mix(0, x, p[0])

    k_p, v_p, lf_p = shared_kv(x)
    lf_t = lf_p.reshape(bsz, seq, HEADS).transpose(0, 2, 1)
    suf_t = forget_suffix(lf_t)
    suf_tok = suf_t.transpose(0, 2, 1).reshape(t, HEADS)
    q = matmul(x, bwq_bf)
    o = fox_prompt(q, k_p, v_p, suf_tok, suf_t, bsz, seq)
    x = matmul_ln(o, bwo_bf, x, ln_mix_g[1], ln_mix_b[1])
    y_prompt = channel_mix(1, x, p[1]).reshape(bsz, seq, D)

    xs = _pad_rows(x_sample.reshape(db, D), DEC_PAD)
    ps = jnp.pad(p_sample.reshape(DEPTH, db, -1), ((0, 0), (0, DEC_PAD - db), (0, 0)))
    qkvo_s = matmul(xs, w_in_bf)
    gates_s = gates_matmul(xs, w_gate, b_gate, HEADS)
    y_s, c_s, n_s, m_s = mlstm_step(qkvo_s[:db].reshape(db, 1, 4 * D), gates_s[:db].reshape(db, 1, LANES),
                                    a_gn_g[0], state_C[0].astype(F32), state_n[0].astype(F32),
                                    state_m[0].astype(F32))
    xs = matmul_ln(_pad_rows(y_s.reshape(db, D), DEC_PAD), w_out_bf, xs, ln_mix_g[0], ln_mix_b[0])
    xs = channel_mix(0, xs, ps[0])

    k_s, v_s, lf_s = shared_kv(xs)
    q_s = matmul(xs, bwq_bf)
    o_s = fox_decode(q_s[:db].reshape(db, HEADS, HD), k_s[:db].reshape(db, HEADS, HD), v_s[:db].reshape(db, HEADS, HD),
                     lf_s[:db], cache_k, cache_v, cache_logf.astype(F32), page_table)
    xs = matmul_ln(_pad_rows(o_s.reshape(db, D), DEC_PAD), bwo_bf, xs, ln_mix_g[1], ln_mix_b[1])
    y_sample = channel_mix(1, xs, ps[1])[:db].reshape(db, 1, D)

    return (y_prompt, y_sample,
            k_p.reshape(bsz, seq, HEADS, HD), v_p.reshape(bsz, seq, HEADS, HD),
            lf_p.reshape(bsz, seq, HEADS).astype(ldt),
            c_p[None].astype(sdt), n_p[None].astype(sdt), m_p[:, :, 0][None].astype(sdt),
            k_s[:db].reshape(db, 1, HEADS, HD), v_s[:db].reshape(db, 1, HEADS, HD),
            lf_s[:db].reshape(db, 1, HEADS).astype(ldt),
            c_s[None].astype(sdt), n_s[None].astype(sdt), m_s[:, 0, :HEADS][None].astype(sdt))
```

```python
import functools
import math

import jax
import jax.numpy as jnp
from jax import lax
from jax.experimental import pallas as pl
from jax.experimental.pallas import tpu as pltpu

F32 = jnp.float32
BF16 = jnp.bfloat16
I32 = jnp.int32

D = 1024
HEADS = 8
HD = 128
DEPTH = 2
PEER_HEADS = 8
PEER_NKEYS = 128
PEER_TOPK = 16
ALPHA = (2.0 * DEPTH) ** 0.25
LN_EPS = 1e-5
HEAD_NORM_EPS = 1e-6
CHUNK = 128
DEC_PAD = 128

LANES = 128
VMEM_LIMIT = 56 * 1024 * 1024

NEG_INF = float("-inf")


def _cparams(sem, vmem=None):
    return pltpu.CompilerParams(dimension_semantics=sem, vmem_limit_bytes=vmem or VMEM_LIMIT)


def _layer_norm(z, g, b):
    mu = jnp.mean(z, axis=-1, keepdims=True)
    zc = z - mu
    var = jnp.mean(zc * zc, axis=-1, keepdims=True)
    return zc * lax.rsqrt(var + LN_EPS) * g + b


def _log_sigmoid(z):
    return jnp.minimum(z, 0.0) - jnp.log1p(jnp.exp(-jnp.abs(z)))


def _split_bf16(a):
    hi = a.astype(BF16)
    lo = (a - hi.astype(F32)).astype(BF16)
    return hi, lo


def _dot(a, b):
    return jnp.dot(a, b, preferred_element_type=F32)


def _dot_nt(a, b):
    return lax.dot_general(a, b, (((1,), (1,)), ((), ())), preferred_element_type=F32)


def _dot3(x, w):
    xh, xl = _split_bf16(x)
    wh, wl = _split_bf16(w)
    return _dot(xh, wh) + _dot(xl, wh) + _dot(xh, wl)


def _mm_kernel(x_ref, w_ref, o_ref, xb_ref):
    @pl.when(pl.program_id(1) == 0)
    def _():
        xb_ref[...] = x_ref[...].astype(BF16)

    o_ref[...] = _dot(xb_ref[...], w_ref[...]).astype(o_ref.dtype)


def matmul(x, w_bf, tm=1024, tn=1024, out_dtype=F32):
    m, k = x.shape
    n = w_bf.shape[1]
    tm = min(tm, m)
    tn = min(tn, n)
    return pl.pallas_call(
        _mm_kernel,
        grid=(m // tm, n // tn),
        in_specs=[pl.BlockSpec((tm, k), lambda i, j: (i, 0)),
                  pl.BlockSpec((k, tn), lambda i, j: (0, j))],
        out_specs=pl.BlockSpec((tm, tn), lambda i, j: (i, j)),
        out_shape=jax.ShapeDtypeStruct((m, n), out_dtype),
        scratch_shapes=[pltpu.VMEM((tm, k), BF16)],
        compiler_params=_cparams(("parallel", "arbitrary")),
        name="matmul",
    )(x, w_bf)


def _gates_kernel(x_ref, w_ref, b_ref, o_ref, *, ls_start):
    z = _dot3(x_ref[...], w_ref[...]) + b_ref[...]
    col = lax.broadcasted_iota(I32, z.shape, 1)
    o_ref[...] = jnp.where(col >= ls_start, _log_sigmoid(z), z)


def gates_matmul(x, w_pad, b_pad, ls_start, tm=512):
    m, k = x.shape
    tm = min(tm, m)
    return pl.pallas_call(
        functools.partial(_gates_kernel, ls_start=ls_start),
        grid=(m // tm,),
        in_specs=[pl.BlockSpec((tm, k), lambda i: (i, 0)),
                  pl.BlockSpec((k, LANES), lambda i: (0, 0)),
                  pl.BlockSpec((1, LANES), lambda i: (0, 0))],
        out_specs=pl.BlockSpec((tm, LANES), lambda i: (i, 0)),
        out_shape=jax.ShapeDtypeStruct((m, LANES), F32),
        compiler_params=_cparams(("parallel",)),
        name="gates_matmul",
    )(x, w_pad, b_pad)


def _mm_ln_kernel(a_ref, w_ref, r_ref, g_ref, b_ref, o_ref):
    y = _dot(a_ref[...].astype(BF16), w_ref[...])
    o_ref[...] = _layer_norm(ALPHA * r_ref[...] + y, g_ref[...], b_ref[...])


def matmul_ln(a, w_bf, res, g, b, tm=512):
    m, k = a.shape
    n = w_bf.shape[1]
    tm = min(tm, m)
    return pl.pallas_call(
        _mm_ln_kernel,
        grid=(m // tm,),
        in_specs=[pl.BlockSpec((tm, k), lambda i: (i, 0)),
                  pl.BlockSpec((k, n), lambda i: (0, 0)),
                  pl.BlockSpec((tm, n), lambda i: (i, 0)),
                  pl.BlockSpec((1, n), lambda i: (0, 0)),
                  pl.BlockSpec((1, n), lambda i: (0, 0))],
        out_specs=pl.BlockSpec((tm, n), lambda i: (i, 0)),
        out_shape=jax.ShapeDtypeStruct((m, n), F32),
        compiler_params=_cparams(("parallel",)),
        name="matmul_ln",
    )(a, w_bf, res, g.reshape(1, n), b.reshape(1, n))


def _ln_ple_kernel(xm_ref, po_ref, g_ref, b_ref, p_ref, wg_ref, bg_ref, wp_ref, o_ref):
    x2 = _layer_norm(ALPHA * xm_ref[...] + po_ref[...], g_ref[...], b_ref[...])
    gate = jax.nn.sigmoid(_dot(x2.astype(BF16), wg_ref[...]) + bg_ref[...])
    emb = _dot(p_ref[...].astype(BF16), wp_ref[...])
    o_ref[...] = x2 + gate * emb


def ln_ple(xm, po, g, b, p, wg_bf, bg, wp_bf, tm=512):
    m, n = xm.shape
    kp = p.shape[1]
    tm = min(tm, m)
    row = lambda i: (i, 0)
    fixed = lambda i: (0, 0)
    return pl.pallas_call(
        _ln_ple_kernel,
        grid=(m // tm,),
        in_specs=[pl.BlockSpec((tm, n), row), pl.BlockSpec((tm, n), row),
                  pl.BlockSpec((1, n), fixed), pl.BlockSpec((1, n), fixed),
                  pl.BlockSpec((tm, kp), row),
                  pl.BlockSpec((n, n), fixed), pl.BlockSpec((1, n), fixed),
                  pl.BlockSpec((kp, n), fixed)],
        out_specs=pl.BlockSpec((tm, n), row),
        out_shape=jax.ShapeDtypeStruct((m, n), F32),
        compiler_params=_cparams(("parallel",)),
        name="ln_ple",
    )(xm, po, g.reshape(1, n), b.reshape(1, n), p, wg_bf, bg.reshape(1, n), wp_bf)


def _ln_kv_kernel(x_ref, g_ref, b_ref, wk_ref, wv_ref, wf_ref, bf_ref, k_ref, v_ref, lf_ref, kb_ref, vb_ref):
    st = _layer_norm(x_ref[...], g_ref[...], b_ref[...])
    sb = st.astype(BF16)
    k = _dot(sb, wk_ref[...])
    v = _dot(sb, wv_ref[...])
    k_ref[...] = k
    v_ref[...] = v
    kb_ref[...] = k.astype(BF16)
    vb_ref[...] = v.astype(BF16)
    lf_ref[...] = _log_sigmoid(_dot3(st, wf_ref[...]) + bf_ref[...])


def ln_kv(x, g, b, wk_bf, wv_bf, wf_pad, bf_pad, tm=512):
    m, n = x.shape
    tm = min(tm, m)
    row = lambda i: (i, 0)
    fixed = lambda i: (0, 0)
    return pl.pallas_call(
        _ln_kv_kernel,
        grid=(m // tm,),
        in_specs=[pl.BlockSpec((tm, n), row), pl.BlockSpec((1, n), fixed), pl.BlockSpec((1, n), fixed),
                  pl.BlockSpec((n, n), fixed), pl.BlockSpec((n, n), fixed),
                  pl.BlockSpec((n, LANES), fixed), pl.BlockSpec((1, LANES), fixed)],
        out_specs=[pl.BlockSpec((tm, n), row), pl.BlockSpec((tm, n), row), pl.BlockSpec((tm, LANES), row),
                   pl.BlockSpec((tm, n), row), pl.BlockSpec((tm, n), row)],
        out_shape=[jax.ShapeDtypeStruct((m, n), F32), jax.ShapeDtypeStruct((m, n), F32),
                   jax.ShapeDtypeStruct((m, LANES), F32),
                   jax.ShapeDtypeStruct((m, n), BF16), jax.ShapeDtypeStruct((m, n), BF16)],
        compiler_params=_cparams(("parallel",)),
        name="ln_kv",
    )(x, g.reshape(1, n), b.reshape(1, n), wk_bf, wv_bf, wf_pad, bf_pad)


def _head_norm_gate(hh, o_pre, gn):
    mu = jnp.mean(hh, axis=-1, keepdims=True)
    hc = hh - mu
    var = jnp.mean(hc * hc, axis=-1, keepdims=True)
    return jax.nn.sigmoid(o_pre) * (hc * lax.rsqrt(var + HEAD_NORM_EPS)) * gn


def _mlstm_prompt_kernel(q_ref, k_ref, v_ref, o_ref, g_ref, gn_ref,
                         y_ref, c_out_ref, n_out_ref, m_out_ref, c_ref, m_ref):
    c_idx = pl.program_id(1)
    L = CHUNK

    @pl.when(c_idx == 0)
    def _():
        c_ref[...] = jnp.zeros_like(c_ref)
        m_ref[...] = jnp.zeros_like(m_ref)

    row = lax.broadcasted_iota(I32, (L, L), 0)
    col = lax.broadcasted_iota(I32, (L, L), 1)
    causal = col <= row
    tril = causal.astype(F32)
    g = g_ref[...]
    cum = jnp.dot(tril, g, preferred_element_type=F32, precision=lax.Precision.HIGHEST)
    g_t = g.T
    cum_t = cum.T
    lane = lax.broadcasted_iota(I32, (L, HD), 1)
    ones_col = (lane == 0).astype(BF16)
    scale = HD ** -0.5

    heads = range(HEADS)
    sls = [slice(h * HD, (h + 1) * HD) for h in heads]
    qbs = [q_ref[:, sl].astype(BF16) for sl in sls]
    kbs = [(k_ref[:, sl] * scale).astype(BF16) for sl in sls]
    c_augs = [c_ref[h] for h in heads]
    qks = [_dot_nt(qbs[h], kbs[h]) for h in heads]
    a_inters = [_dot(qbs[h], c_augs[h].astype(BF16)) for h in heads]

    pre = []
    for h in heads:
        li_c = g[:, h:h + 1]
        b_c = cum[:, HEADS + h:HEADS + h + 1]
        m_prev = m_ref[h:h + 1, 0:1]
        dmat = jnp.where(causal, b_c - cum_t[HEADS + h:HEADS + h + 1, :] + g_t[h:h + 1, :], NEG_INF)
        b_last = b_c[L - 1:L, :]
        g_c = b_last - b_c + li_c
        pre.append((dmat, jnp.max(dmat, axis=1, keepdims=True), b_c + m_prev, g_c,
                    jnp.max(g_c, axis=0, keepdims=True), b_last + m_prev))

    stats, wide = [], []
    for h in heads:
        dmat, dmax, inter, g_c, gmax, carry = pre[h]
        m_t = jnp.maximum(inter, dmax)
        w_inter = jnp.exp(inter - m_t)
        m_new = jnp.maximum(carry, gmax)
        decay = jnp.exp(carry - m_new)
        w_c = jnp.exp(g_c - m_new)
        stats.append((m_t, w_inter, m_new, decay))
        wide.append((jnp.broadcast_to(m_t, (L, L)), jnp.broadcast_to(w_c, (L, HD))))

    ps, vws = [], []
    for h in heads:
        m_t_wide, w_c_wide = wide[h]
        ps.append((qks[h] * jnp.exp(pre[h][0] - m_t_wide)).astype(BF16))
        vws.append(jnp.concatenate([v_ref[:, sls[h]] * w_c_wide, jnp.where(lane == 0, w_c_wide, 0.0)],
                                   axis=1).astype(BF16))

    a_intras = [_dot(ps[h], jnp.concatenate([v_ref[:, sls[h]].astype(BF16), ones_col], axis=1)) for h in heads]
    upds = [lax.dot_general(kbs[h], vws[h], (((0,), (0,)), ((), ())), preferred_element_type=F32) for h in heads]

    hhs, mus = [], []
    for h in heads:
        m_t, w_inter, m_new, decay = stats[h]
        a_inter, a_intra = a_inters[h], a_intras[h]
        num = w_inter * a_inter[:, :HD] + a_intra[:, :HD]
        den = w_inter * a_inter[:, HD:HD + 1] + a_intra[:, HD:HD + 1]
        hh = num / jnp.maximum(jnp.abs(den), jnp.exp(-m_t))
        hhs.append(hh)
        mus.append(jnp.mean(hh, axis=-1, keepdims=True))
        c_ref[h] = decay * c_augs[h] + upds[h]
        m_ref[h:h + 1, :] = jnp.broadcast_to(m_new, (1, LANES))
    hcs = [hhs[h] - mus[h] for h in heads]
    variances = [jnp.mean(hc * hc, axis=-1, keepdims=True) for hc in hcs]
    for h in heads:
        hn = hcs[h] * lax.rsqrt(variances[h] + HEAD_NORM_EPS)
        y_ref[:, sls[h]] = jax.nn.sigmoid(o_ref[:, sls[h]]) * hn * gn_ref[:, sls[h]]

    @pl.when(c_idx == pl.num_programs(1) - 1)
    def _():
        for h in range(HEADS):
            c_aug = c_ref[h]
            c_out_ref[0, h] = c_aug[:, :HD]
            n_out_ref[0, h:h + 1, :] = c_aug[:, HD:].T[0:1, :]
        m_out_ref[0] = m_ref[...]


def mlstm_prompt(qkvo, gates, gn_g, bsz, seq):
    nc = seq // CHUNK
    t = bsz * seq

    def col_spec(gidx):
        return pl.BlockSpec((CHUNK, D), lambda b, c: (b * nc + c, gidx))

    return pl.pallas_call(
        _mlstm_prompt_kernel,
        grid=(bsz, nc),
        in_specs=[col_spec(0), col_spec(1), col_spec(2), col_spec(3),
                  pl.BlockSpec((CHUNK, LANES), lambda b, c: (b * nc + c, 0)),
                  pl.BlockSpec((1, D), lambda b, c: (0, 0))],
        out_specs=[pl.BlockSpec((CHUNK, D), lambda b, c: (b * nc + c, 0)),
                   pl.BlockSpec((1, HEADS, HD, HD), lambda b, c: (b, 0, 0, 0)),
                   pl.BlockSpec((1, HEADS, HD), lambda b, c: (b, 0, 0)),
                   pl.BlockSpec((1, HEADS, LANES), lambda b, c: (b, 0, 0))],
        out_shape=[jax.ShapeDtypeStruct((t, D), F32),
                   jax.ShapeDtypeStruct((bsz, HEADS, HD, HD), F32),
                   jax.ShapeDtypeStruct((bsz, HEADS, HD), F32),
                   jax.ShapeDtypeStruct((bsz, HEADS, LANES), F32)],
        scratch_shapes=[pltpu.VMEM((HEADS, HD, 2 * HD), F32), pltpu.VMEM((HEADS, LANES), F32)],
        compiler_params=_cparams(("parallel", "arbitrary")),
        name="mlstm_prompt",
    )(qkvo, qkvo, qkvo, qkvo, gates, gn_g.reshape(1, D))


def _mlstm_step_kernel(q_ref, k_ref, v_ref, o_ref, g_ref, gn_ref, c0_ref, n0_ref, m0_ref,
                       y_ref, c_out_ref, n_out_ref, m_out_ref):
    row = lax.broadcasted_iota(I32, (HD, HD), 0)
    col = lax.broadcasted_iota(I32, (HD, HD), 1)
    eye = row == col
    scale = HD ** -0.5
    g = g_ref[0]
    m0 = m0_ref[0]
    m_new_all = jnp.zeros((1, LANES), F32)
    lane = lax.broadcasted_iota(I32, (1, LANES), 1)

    def to_col(r):
        return jnp.sum(jnp.where(eye, r, 0.0), axis=1, keepdims=True)

    for h in range(HEADS):
        sl = slice(h * HD, (h + 1) * HD)
        q = q_ref[0][:, sl]
        ks = k_ref[0][:, sl] * scale
        v = v_ref[0][:, sl]
        li = g[:, h:h + 1]
        lf = g[:, HEADS + h:HEADS + h + 1]
        m_prev = m0[:, h:h + 1]
        c0 = c0_ref[0, h]
        n0 = n0_ref[0, h:h + 1, :]
        inter = lf + m_prev
        m_t = jnp.maximum(inter, li)
        w_inter = jnp.exp(inter - m_t)
        p = jnp.exp(li - m_t)
        qk = jnp.sum(q * ks, axis=1, keepdims=True) * p
        q_c = jnp.sum(to_col(q) * c0, axis=0, keepdims=True)
        q_n = jnp.sum(q * n0, axis=1, keepdims=True)
        num = w_inter * q_c + qk * v
        den = w_inter * q_n + qk
        hh = num / jnp.maximum(jnp.abs(den), jnp.exp(-m_t))
        y_ref[0, :, sl] = _head_norm_gate(hh, o_ref[0][:, sl], gn_ref[:, sl])
        decay = w_inter
        c_out_ref[0, h] = decay * c0 + to_col(p * ks) * v
        n_out_ref[0, h:h + 1, :] = decay * n0 + p * ks
        m_new_all = jnp.where(lane == h, m_t, m_new_all)
    m_out_ref[0] = m_new_all


def mlstm_step(qkvo3, gates3, gn_g, c0, n0, m0):
    nb = c0.shape[0]

    def col_spec(gidx):
        return pl.BlockSpec((1, 1, D), lambda b: (b, 0, gidx))

    return pl.pallas_call(
        _mlstm_step_kernel,
        grid=(nb,),
        in_specs=[col_spec(0), col_spec(1), col_spec(2), col_spec(3),
                  pl.BlockSpec((1, 1, LANES), lambda b: (b, 0, 0)),
                  pl.BlockSpec((1, D), lambda b: (0, 0)),
                  pl.BlockSpec((1, HEADS, HD, HD), lambda b: (b, 0, 0, 0)),
                  pl.BlockSpec((1, HEADS, HD), lambda b: (b, 0, 0)),
                  pl.BlockSpec((1, 1, HEADS), lambda b: (b, 0, 0))],
        out_specs=[pl.BlockSpec((1, 1, D), lambda b: (b, 0, 0)),
                   pl.BlockSpec((1, HEADS, HD, HD), lambda b: (b, 0, 0, 0)),
                   pl.BlockSpec((1, HEADS, HD), lambda b: (b, 0, 0)),
                   pl.BlockSpec((1, 1, LANES), lambda b: (b, 0, 0))],
        out_shape=[jax.ShapeDtypeStruct((nb, 1, D), F32),
                   jax.ShapeDtypeStruct((nb, HEADS, HD, HD), F32),
                   jax.ShapeDtypeStruct((nb, HEADS, HD), F32),
                   jax.ShapeDtypeStruct((nb, 1, LANES), F32)],
        compiler_params=_cparams(("parallel",)),
        name="mlstm_step",
    )(qkvo3, qkvo3, qkvo3, qkvo3, gates3, gn_g.reshape(1, D), c0, n0, m0.reshape(nb, 1, HEADS))


def _suffix_kernel(lf_ref, o_ref):
    s = lf_ref.shape[2]
    row = lax.broadcasted_iota(I32, (LANES, LANES), 0)
    col = lax.broadcasted_iota(I32, (LANES, LANES), 1)
    upper = (row > col).astype(F32)
    carry = jnp.zeros((HEADS, 1), F32)
    for c in reversed(range(s // LANES)):
        x = lf_ref[0, :, c * LANES:(c + 1) * LANES]
        inner = jnp.dot(x, upper, preferred_element_type=F32, precision=lax.Precision.HIGHEST)
        o_ref[0, :, c * LANES:(c + 1) * LANES] = inner + carry
        carry = carry + jnp.sum(x, axis=1, keepdims=True)


def forget_suffix(lf_t):
    bsz, nh, s = lf_t.shape
    return pl.pallas_call(
        _suffix_kernel,
        grid=(bsz,),
        in_specs=[pl.BlockSpec((1, nh, s), lambda b: (b, 0, 0))],
        out_specs=pl.BlockSpec((1, nh, s), lambda b: (b, 0, 0)),
        out_shape=jax.ShapeDtypeStruct((bsz, nh, s), F32),
        compiler_params=_cparams(("parallel",)),
        name="forget_suffix",
    )(lf_t)


def _fox_prompt_kernel(q_ref, k_ref, v_ref, sq_ref, sk_ref, o_ref, acc_ref, m_ref, l_ref, *, tq):
    i = pl.program_id(1)
    j = pl.program_id(2)
    scale = HD ** -0.5

    @pl.when(j == 0)
    def _():
        acc_ref[...] = jnp.zeros_like(acc_ref)
        m_ref[...] = jnp.full_like(m_ref, NEG_INF)
        l_ref[...] = jnp.zeros_like(l_ref)

    def update(on_diagonal):
        sq = sq_ref[...]
        sk = sk_ref[0]
        if on_diagonal:
            visible = lax.broadcasted_iota(I32, (tq, tq), 1) <= lax.broadcasted_iota(I32, (tq, tq), 0)
        for h in range(HEADS):
            sl = slice(h * HD, (h + 1) * HD)
            sqb = jnp.broadcast_to(sq[:, h:h + 1], (tq, LANES))
            s = _dot_nt(q_ref[:, sl].astype(BF16), k_ref[:, sl].astype(BF16)) * scale + sk[h:h + 1, :]
            if on_diagonal:
                s = jnp.where(visible, s, NEG_INF)
            m_prev = m_ref[h]
            m_new = jnp.maximum(m_prev, jnp.max(s, axis=1, keepdims=True) - sqb)
            alpha = jnp.exp(m_prev - m_new)
            p = jnp.exp(s - jnp.concatenate([m_new + sqb] * (tq // LANES), axis=1))
            l_ref[h] = alpha * l_ref[h] + jnp.sum(p, axis=1, keepdims=True)
            acc_ref[:, sl] = alpha * acc_ref[:, sl] + _dot(p.astype(BF16), v_ref[:, sl].astype(BF16))
            m_ref[h] = m_new

    @pl.when(j < i)
    def _():
        update(False)

    @pl.when(j == i)
    def _():
        update(True)
        for h in range(HEADS):
            sl = slice(h * HD, (h + 1) * HD)
            o_ref[:, sl] = acc_ref[:, sl] / l_ref[h]


def fox_prompt(q, k, v, suf_tok, suf_t, bsz, seq, tq=512):
    tq = min(tq, seq)
    nq = seq // tq
    t = bsz * seq
    kv_spec = pl.BlockSpec((tq, D), lambda b, i, j: (b * nq + jnp.minimum(j, i), 0))
    return pl.pallas_call(
        functools.partial(_fox_prompt_kernel, tq=tq),
        grid=(bsz, nq, nq),
        in_specs=[pl.BlockSpec((tq, D), lambda b, i, j: (b * nq + i, 0)),
                  kv_spec, kv_spec,
                  pl.BlockSpec((tq, HEADS), lambda b, i, j: (b * nq + i, 0)),
                  pl.BlockSpec((1, HEADS, tq), lambda b, i, j: (b, 0, jnp.minimum(j, i)))],
        out_specs=pl.BlockSpec((tq, D), lambda b, i, j: (b * nq + i, 0)),
        out_shape=jax.ShapeDtypeStruct((t, D), F32),
        scratch_shapes=[pltpu.VMEM((tq, D), F32), pltpu.VMEM((HEADS, tq, LANES), F32),
                        pltpu.VMEM((HEADS, tq, LANES), F32)],
        compiler_params=_cparams(("parallel", "parallel", "arbitrary")),
        name="fox_prompt",
    )(q, k, v, suf_tok, suf_t)


def _page_suffix_kernel(lf_ref, suf_ref, tot_ref):
    w = lf_ref.shape[1]
    row = lax.broadcasted_iota(I32, (w, w), 0)
    col = lax.broadcasted_iota(I32, (w, w), 1)
    same_head = (row % HEADS) == (col % HEADS)
    later = (same_head & (row > col)).astype(BF16)
    whole = same_head.astype(BF16)
    x = lf_ref[...]
    hi = x.astype(BF16)
    r1 = x - hi.astype(F32)
    mid = r1.astype(BF16)
    lo = (r1 - mid.astype(F32)).astype(BF16)
    suf_ref[...] = _dot(hi, later) + _dot(mid, later) + _dot(lo, later)
    tot_ref[...] = _dot(hi, whole) + _dot(mid, whole) + _dot(lo, whole)


def page_suffix(lf_flat, tr=512):
    r, w = lf_flat.shape
    tr = math.gcd(r, tr)
    spec = pl.BlockSpec((tr, w), lambda i: (i, 0))
    return pl.pallas_call(
        _page_suffix_kernel,
        grid=(r // tr,),
        in_specs=[spec],
        out_specs=[spec, spec],
        out_shape=[jax.ShapeDtypeStruct((r, w), F32), jax.ShapeDtypeStruct((r, w), F32)],
        compiler_params=_cparams(("parallel",)),
        name="page_suffix",
    )(lf_flat)


def _fox_decode_kernel(pt_ref, q_ref, kn_ref, vn_ref, lfn_ref, *refs, pg):
    page_refs = refs[:4 * pg]
    o_ref, acc_ref, m_ref, l_ref, r_ref = refs[4 * pg:]
    j = pl.program_id(1)
    scale = HD ** -0.5
    w = r_ref.shape[1]
    q8 = q_ref[0]

    @pl.when(j == 0)
    def _():
        s_new = jnp.sum(q8 * kn_ref[0], axis=1, keepdims=True) * scale
        m_ref[...] = jnp.broadcast_to(s_new, m_ref.shape)
        l_ref[...] = jnp.ones_like(l_ref)
        acc_ref[...] = vn_ref[0]
        r_ref[...] = lfn_ref[0]

    run = r_ref[...]
    bias, ks, vs = [], [], []
    for i in range(pg):
        k_ref, v_ref, suf_ref, tot_ref = page_refs[4 * i:4 * i + 4]
        bias.append(suf_ref[0] + run)
        run = run + tot_ref[0]
        ks.append(k_ref[0].reshape(w, HD).astype(BF16))
        vs.append(v_ref[0].reshape(w, HD).astype(BF16))
    r_ref[...] = run
    own = (lax.broadcasted_iota(I32, (HEADS, pg * w), 1) % HEADS) == lax.broadcasted_iota(I32, (HEADS, pg * w), 0)
    s = _dot_nt(q8.astype(BF16), jnp.concatenate(ks, axis=0)) * scale + jnp.concatenate(bias, axis=1)
    s = jnp.where(own, s, NEG_INF)
    m_prev = m_ref[:, 0:1]
    m_new = jnp.maximum(m_prev, jnp.max(s, axis=1, keepdims=True))
    alpha = jnp.exp(m_prev - m_new)
    p = jnp.exp(s - m_new)
    l_ref[...] = alpha * l_ref[...] + jnp.sum(p, axis=1, keepdims=True)
    acc_ref[...] = alpha * acc_ref[...] + _dot(p.astype(BF16), jnp.concatenate(vs, axis=0))
    m_ref[...] = jnp.broadcast_to(m_new, m_ref.shape)

    @pl.when(j == pl.num_programs(1) - 1)
    def _():
        o_ref[0] = acc_ref[...] / l_ref[...]


def fox_decode(q, k, v, lf_new, cache_k, cache_v, cache_logf, page_table, pages_per_step=8):
    nb, n_pages = page_table.shape
    n_pool, page = cache_k.shape[:2]
    w = page * HEADS
    pg = math.gcd(n_pages, pages_per_step)
    suf_in, tot = page_suffix(cache_logf.reshape(n_pool, w))
    suf_in = suf_in.reshape(n_pool, 1, w)
    tot = tot.reshape(n_pool, 1, w)
    lfn = jnp.tile(lf_new, (1, page)).reshape(nb, 1, w)
    tok = lambda b, j, pt: (b, 0, 0)
    in_specs = [pl.BlockSpec((1, HEADS, HD), tok), pl.BlockSpec((1, HEADS, HD), tok), pl.BlockSpec((1, HEADS, HD), tok),
                pl.BlockSpec((1, 1, w), tok)]
    operands = []
    for i in range(pg):
        sel4 = lambda b, j, pt, i=i: (pt[b, n_pages - 1 - (j * pg + i)], 0, 0, 0)
        sel3 = lambda b, j, pt, i=i: (pt[b, n_pages - 1 - (j * pg + i)], 0, 0)
        in_specs += [pl.BlockSpec((1, page, HEADS, HD), sel4), pl.BlockSpec((1, page, HEADS, HD), sel4),
                     pl.BlockSpec((1, 1, w), sel3), pl.BlockSpec((1, 1, w), sel3)]
        operands += [cache_k, cache_v, suf_in, tot]
    grid_spec = pltpu.PrefetchScalarGridSpec(
        num_scalar_prefetch=1,
        grid=(nb, n_pages // pg),
        in_specs=in_specs,
        out_specs=pl.BlockSpec((1, HEADS, HD), tok),
        scratch_shapes=[pltpu.VMEM((HEADS, HD), F32), pltpu.VMEM((HEADS, LANES), F32),
                        pltpu.VMEM((HEADS, LANES), F32), pltpu.VMEM((1, w), F32)],
    )
    return pl.pallas_call(
        functools.partial(_fox_decode_kernel, pg=pg),
        grid_spec=grid_spec,
        out_shape=jax.ShapeDtypeStruct((nb, HEADS, HD), F32),
        compiler_params=_cparams(("parallel", "arbitrary")),
        name="fox_decode",
    )(page_table, q, k, v, lfn, *operands)


PICKS = PEER_HEADS * PEER_TOPK
EXPERT_BITS = 14
PAIR_ROWS = 8
TILE_ROWS = 2 * PAIR_ROWS
GROWS = PICKS * TILE_ROWS


def _take_topk(s, tie, k, put):
    big = 3.0e38
    for r in range(k):
        m = jnp.max(s, axis=0, keepdims=True)
        pos = jnp.min(jnp.where(s == m, tie, big), axis=0, keepdims=True)
        put(r, m, pos)
        s = jnp.where(tie == pos, NEG_INF, s)


def _take_topk_keys(s, k, put):
    g = s.shape[0] // 8
    base = lax.broadcasted_iota(I32, (8, s.shape[1]), 0).astype(F32)
    vals = [s[8 * v:8 * v + 8, :] for v in range(g)]
    ids = [base + float(8 * v) for v in range(g)]
    for rnd in range(g):
        for i in range(rnd % 2, g - 1, 2):
            swap = vals[i + 1] > vals[i]
            vals[i], vals[i + 1] = jnp.where(swap, vals[i + 1], vals[i]), jnp.where(swap, vals[i], vals[i + 1])
            ids[i], ids[i + 1] = jnp.where(swap, ids[i + 1], ids[i]), jnp.where(swap, ids[i], ids[i + 1])
    big = 3.0e38
    for r in range(k):
        m = jnp.max(vals[0], axis=0, keepdims=True)
        pos = jnp.min(jnp.where(vals[0] == m, ids[0], big), axis=0, keepdims=True)
        put(r, m, pos)
        taken = ids[0] == pos
        last = min(g, k) - 1 - r
        for v in range(last):
            vals[v] = jnp.where(taken, vals[v + 1], vals[v])
            ids[v] = jnp.where(taken, ids[v + 1], ids[v])
        if last >= 0:
            vals[last] = jnp.where(taken, NEG_INF, vals[last])


def _candidate_bins(kk):
    full, small = [], []
    for a in range(kk):
        nb = kk // (a + 1)
        b0 = 0
        while nb - b0 >= 8:
            full.append([(a, b0, 0, 8)])
            b0 += 8
        if nb > b0:
            small.append((a, b0, nb - b0))
    bins = []
    for a, b0, rows in sorted(small, key=lambda piece: -piece[2]):
        for group in bins:
            used = group[-1][2] + group[-1][3]
            if used + rows <= 8:
                group.append((a, b0, used, rows))
                break
        else:
            bins.append([(a, b0, 0, rows)])
    return full + bins


def _peer_select_kernel(x_ref, wq_ref, sk_ref, idx_ref, off_ref, g_ref, sc_ref, sv_ref, si_ref, e_ref, gg_ref):
    tb = x_ref.shape[0]
    kk = PEER_TOPK
    nk = PEER_NKEYS
    lw = LANES
    q = _dot(x_ref[...].astype(BF16), wq_ref[...]).astype(BF16)
    sub8 = lax.broadcasted_iota(I32, (8, lw), 0)
    sub8f = sub8.astype(F32)
    bins = _candidate_bins(kk)

    for h in range(PEER_HEADS):
        for c in range(2):
            qc = q[:, (2 * h + c) * nk:(2 * h + c + 1) * nk]
            sc_ref[c] = _dot_nt(sk_ref[c], qc)
        for part in range(tb // lw):
            ls = slice(part * lw, (part + 1) * lw)
            for c in range(2):
                def put1(r, m, pos, c=c):
                    sv_ref[c, r:r + 1, ls] = m
                    si_ref[c, r:r + 1, ls] = pos

                _take_topk_keys(sc_ref[c, :, ls], kk, put1)

            cand, ckey = [], []
            for pieces in bins:
                va = ia = vb = ib = flat = None
                used = 0
                for a, b0, start, rows in pieces:
                    ra = jnp.broadcast_to(sv_ref[0, a:a + 1, ls], (8, lw))
                    ri = jnp.broadcast_to(si_ref[0, a:a + 1, ls], (8, lw))
                    rb = sv_ref[1, b0:b0 + 8, ls]
                    rj = si_ref[1, b0:b0 + 8, ls]
                    rf = sub8f + float(a * kk + b0 - start)
                    if start:
                        rb = pltpu.roll(rb, start, axis=0)
                        rj = pltpu.roll(rj, start, axis=0)
                        here = sub8 >= start
                        va, ia, vb = jnp.where(here, ra, va), jnp.where(here, ri, ia), jnp.where(here, rb, vb)
                        ib, flat = jnp.where(here, rj, ib), jnp.where(here, rf, flat)
                    else:
                        va, ia, vb, ib, flat = ra, ri, rb, rj, rf
                    used = start + rows
                c8 = va + vb
                cand.append(c8 if used == 8 else jnp.where(sub8 < used, c8, NEG_INF))
                ckey.append(flat * float(2 ** EXPERT_BITS) + (ia * float(nk) + ib))
            cand = jnp.concatenate(cand, axis=0)
            ckey = jnp.concatenate(ckey, axis=0)

            def put2(r, m, pos, h=h):
                gg_ref[h * kk + r:h * kk + r + 1, ls] = m
                e_ref[h * kk + r:h * kk + r + 1, ls] = pos

            _take_topk(cand, ckey, kk, put2)
            cv = gg_ref[h * kk:(h + 1) * kk, ls]
            ex = jnp.exp(cv - cv[0:1, :])
            gg_ref[h * kk:(h + 1) * kk, ls] = ex / jnp.sum(ex, axis=0, keepdims=True)

    expert = e_ref[...].T.astype(I32) & (2 ** EXPERT_BITS - 1)
    idx_ref[...] = expert
    off_ref[...] = (expert >> 1) * PAIR_ROWS
    g_ref[...] = gg_ref[...].T


def peer_select(x, wq_bf, sk_bf, tb=256):
    m, n = x.shape
    nq = wq_bf.shape[1]
    tb = min(tb, m)
    out_spec = pl.BlockSpec((tb, PICKS), lambda i: (i, 0))
    return pl.pallas_call(
        _peer_select_kernel,
        grid=(m // tb,),
        in_specs=[pl.BlockSpec((tb, n), lambda i: (i, 0)),
                  pl.BlockSpec((n, nq), lambda i: (0, 0)),
                  pl.BlockSpec((2, PEER_NKEYS, PEER_NKEYS), lambda i: (0, 0, 0))],
        out_specs=[out_spec, out_spec, out_spec],
        out_shape=[jax.ShapeDtypeStruct((m, PICKS), I32), jax.ShapeDtypeStruct((m, PICKS), I32),
                   jax.ShapeDtypeStruct((m, PICKS), F32)],
        scratch_shapes=[pltpu.VMEM((2, PEER_NKEYS, tb), F32),
                        pltpu.VMEM((2, PEER_TOPK, tb), F32), pltpu.VMEM((2, PEER_TOPK, tb), F32),
                        pltpu.VMEM((PICKS, tb), F32), pltpu.VMEM((PICKS, tb), F32)],
        compiler_params=_cparams(("parallel",)),
        name="peer_select",
    )(x, wq_bf, sk_bf)


def _diag_mask():
    sub = lax.broadcasted_iota(I32, (8, GROWS), 0)
    lane = lax.broadcasted_iota(I32, (8, GROWS), 1)
    return (lane % TILE_ROWS) // 2 == sub


def _gather_pairs(off_smem, tab_ref, t):
    tiles = []
    for p in range(PICKS):
        off = pl.multiple_of(off_smem[t, p], PAIR_ROWS)
        tiles.append(pltpu.bitcast(tab_ref[pl.ds(off, PAIR_ROWS), :], BF16))
    return jnp.concatenate(tiles, axis=0)


TOKEN_UNROLL = 32


def _peer_up_kernel(off_smem, x_ref, idxv_ref, gate_ref, tab_ref, ecomp_ref, w_ref, a_ref):
    tbu = x_ref.shape[0]
    mdiag = _diag_mask()

    def body(t, carry):
        g = _gather_pairs(off_smem, tab_ref, t)
        xr = x_ref[pl.ds(t, 1), :]
        x8 = jnp.concatenate([xr[:, s * LANES:(s + 1) * LANES] for s in range(D // LANES)], axis=0)
        res = _dot_nt(x8.astype(BF16), g)
        a_ref[pl.ds(t, 1), :] = jnp.sum(jnp.where(mdiag, res, 0.0), axis=0, keepdims=True)
        return carry

    lax.fori_loop(0, tbu, body, 0, unroll=TOKEN_UNROLL)
    hi, lo = _split_bf16(a_ref[...])
    both = _dot(hi, ecomp_ref[...]) + _dot(lo, ecomp_ref[...])
    odd = (idxv_ref[...] & 1) == 1
    a = jnp.where(odd, both[:, PICKS:], both[:, :PICKS])
    w_ref[...] = gate_ref[...] * (0.5 * a * (1.0 + lax.erf(a * (2.0 ** -0.5))))


def _peer_down_kernel(off_smem, w_ref, idxv_ref, tab_ref, eexp_ref, o_ref, wx_ref):
    tbu = w_ref.shape[0]
    mdiag = _diag_mask()
    w = w_ref[...]
    odd = (idxv_ref[...] & 1) == 1
    w01 = jnp.concatenate([jnp.where(odd, 0.0, w), jnp.where(odd, w, 0.0)], axis=1).astype(BF16)
    wx_ref[...] = _dot(w01, eexp_ref[...])

    def body(t, carry):
        g = _gather_pairs(off_smem, tab_ref, t)
        wm = jnp.where(mdiag, wx_ref[pl.ds(t, 1), :], 0.0).astype(BF16)
        o_ref[t] = _dot(wm, g)
        return carry

    lax.fori_loop(0, tbu, body, 0, unroll=TOKEN_UNROLL)


def _expand_matrix():
    r = lax.broadcasted_iota(I32, (2 * PICKS, GROWS), 0)
    k = lax.broadcasted_iota(I32, (2 * PICKS, GROWS), 1)
    return ((k // TILE_ROWS == r % PICKS) & (k % 2 == r // PICKS)).astype(BF16)


def _table_spec(tab):
    return pl.BlockSpec(tab.shape, lambda i: (0, 0), pipeline_mode=pl.Buffered(1))


def peer_up(off, idx, x, gate, tab, tbu=128):
    m = x.shape[0]
    tbu = min(tbu, m)
    ecomp = _expand_matrix().T
    return pl.pallas_call(
        _peer_up_kernel,
        grid=(m // tbu,),
        in_specs=[pl.BlockSpec((tbu, PICKS), lambda i: (i, 0), memory_space=pltpu.SMEM),
                  pl.BlockSpec((tbu, D), lambda i: (i, 0)),
                  pl.BlockSpec((tbu, PICKS), lambda i: (i, 0)),
                  pl.BlockSpec((tbu, PICKS), lambda i: (i, 0)),
                  _table_spec(tab),
                  pl.BlockSpec((GROWS, 2 * PICKS), lambda i: (0, 0))],
        out_specs=pl.BlockSpec((tbu, PICKS), lambda i: (i, 0)),
        out_shape=jax.ShapeDtypeStruct((m, PICKS), F32),
        scratch_shapes=[pltpu.VMEM((tbu, GROWS), F32)],
        compiler_params=_cparams(("arbitrary",)),
        name="peer_up",
    )(off, x, idx, gate, tab, ecomp)


def peer_down(off, idx, w, tab, tbu=128):
    m = w.shape[0]
    tbu = min(tbu, m)
    eexp = _expand_matrix()
    return pl.pallas_call(
        _peer_down_kernel,
        grid=(m // tbu,),
        in_specs=[pl.BlockSpec((tbu, PICKS), lambda i: (i, 0), memory_space=pltpu.SMEM),
                  pl.BlockSpec((tbu, PICKS), lambda i: (i, 0)),
                  pl.BlockSpec((tbu, PICKS), lambda i: (i, 0)),
                  _table_spec(tab),
                  pl.BlockSpec((2 * PICKS, GROWS), lambda i: (0, 0))],
        out_specs=pl.BlockSpec((tbu, 8, LANES), lambda i: (i, 0, 0)),
        out_shape=jax.ShapeDtypeStruct((m, 8, LANES), F32),
        scratch_shapes=[pltpu.VMEM((tbu, GROWS), F32)],
        compiler_params=_cparams(("arbitrary",)),
        name="peer_down",
    )(off, w, idx, tab, eexp).reshape(m, D)


def _pack_table(tab):
    e = tab.shape[0]
    bits = lax.bitcast_convert_type(tab.astype(BF16), jnp.uint16).astype(jnp.uint32).reshape(e // 2, 2, D)
    words = bits[:, 0, :] | (bits[:, 1, :] << 16)
    return words.reshape(e // 2 * PAIR_ROWS, LANES)


def peer(x, wq_bf, sk_bf, u_tab, v_tab):
    idx, off, gate = peer_select(x, wq_bf, sk_bf)
    w = peer_up(off, idx, x, gate, u_tab)
    return peer_down(off, idx, w, v_tab)


def _pad_cols(w, n=LANES):
    return jnp.pad(w, ((0, 0), (0, n - w.shape[1])))


def _pad_rows(a, rows):
    return jnp.pad(a, ((0, rows - a.shape[0]),) + ((0, 0),) * (a.ndim - 1))


def kernel(x_prompt, x_sample, cache_k, cache_v, cache_logf, state_C, state_n, state_m, page_table, p_prompt, p_sample, a_w_in, a_b_gate, a_gn_g, a_w_out, kv_ln_g, kv_ln_b, kv_w, kv_b_f, b_w_q, b_w_o, ln_mix_g, ln_mix_b, ln_ffn_g, ln_ffn_b, peer_w_q, peer_subkeys, peer_u, peer_v, ple_w_p, ple_w_g, ple_b_g):
    bsz, seq, _ = x_prompt.shape
    db = x_sample.shape[0]
    n_pool, page = cache_k.shape[:2]
    sdt = state_C.dtype
    ldt = cache_logf.dtype

    w_in_bf = a_w_in[0][:, :4 * D].astype(BF16)
    w_gate = _pad_cols(a_w_in[0][:, 4 * D:])
    b_gate = _pad_cols(a_b_gate[0].reshape(1, -1))
    w_out_bf = a_w_out[0].astype(BF16)
    wk_bf = kv_w[:, :D].astype(BF16)
    wv_bf = kv_w[:, D:2 * D].astype(BF16)
    wf = _pad_cols(kv_w[:, 2 * D:])
    bf = _pad_cols(kv_b_f.reshape(1, -1))
    bwq_bf = b_w_q[0].astype(BF16)
    bwo_bf = b_w_o[0].astype(BF16)
    pwq_bf = peer_w_q.astype(BF16)
    sk_bf = peer_subkeys.astype(BF16)
    u_tabs = [_pack_table(peer_u[i]) for i in range(DEPTH)]
    v_tabs = [_pack_table(peer_v[i]) for i in range(DEPTH)]
    wg_bf = ple_w_g.astype(BF16)
    wp_bf = ple_w_p.astype(BF16)

    def channel_mix(i, x_mid, p):
        po = peer(x_mid, pwq_bf[i], sk_bf[i], u_tabs[i], v_tabs[i])
        return ln_ple(x_mid, po, ln_ffn_g[i], ln_ffn_b[i], p, wg_bf[i], ple_b_g[i], wp_bf[i])

    def shared_kv(x):
        k, v, lfp, k_bf, v_bf = ln_kv(x, kv_ln_g, kv_ln_b, wk_bf, wv_bf, wf, bf)
        return k, v, lfp[:, :HEADS], k_bf, v_bf

    t = bsz * seq
    x = x_prompt.reshape(t, D)
    p = p_prompt.reshape(DEPTH, t, -1)
    qkvo = matmul(x, w_in_bf)
    gates = gates_matmul(x, w_gate, b_gate, HEADS)
    y_pre, c_p, n_p, m_p = mlstm_prompt(qkvo, gates, a_gn_g[0], bsz, seq)
    x = matmul_ln(y_pre, w_out_bf, x, ln_mix_g[0], ln_mix_b[0])
    x = channel_mix(0, x, p[0])

    k_p, v_p, lf_p, k_pb, v_pb = shared_kv(x)
    lf_t = lf_p.reshape(bsz, seq, HEADS).transpose(0, 2, 1)
    suf_t = forget_suffix(lf_t)
    suf_tok = suf_t.transpose(0, 2, 1).reshape(t, HEADS)
    q = matmul(x, bwq_bf, out_dtype=BF16)
    o = fox_prompt(q, k_pb, v_pb, suf_tok, suf_t, bsz, seq)
    x = matmul_ln(o, bwo_bf, x, ln_mix_g[1], ln_mix_b[1])
    y_prompt = channel_mix(1, x, p[1]).reshape(bsz, seq, D)

    xs = _pad_rows(x_sample.reshape(db, D), DEC_PAD)
    ps = jnp.pad(p_sample.reshape(DEPTH, db, -1), ((0, 0), (0, DEC_PAD - db), (0, 0)))
    qkvo_s = matmul(xs, w_in_bf)
    gates_s = gates_matmul(xs, w_gate, b_gate, HEADS)
    y_s, c_s, n_s, m_s = mlstm_step(qkvo_s[:db].reshape(db, 1, 4 * D), gates_s[:db].reshape(db, 1, LANES),
                                    a_gn_g[0], state_C[0].astype(F32), state_n[0].astype(F32),
                                    state_m[0].astype(F32))
    xs = matmul_ln(_pad_rows(y_s.reshape(db, D), DEC_PAD), w_out_bf, xs, ln_mix_g[0], ln_mix_b[0])
    xs = channel_mix(0, xs, ps[0])

    k_s, v_s, lf_s, _, _ = shared_kv(xs)
    q_s = matmul(xs, bwq_bf)
    o_s = fox_decode(q_s[:db].reshape(db, HEADS, HD), k_s[:db].reshape(db, HEADS, HD), v_s[:db].reshape(db, HEADS, HD),
                     lf_s[:db], cache_k, cache_v, cache_logf.astype(F32), page_table)
    xs = matmul_ln(_pad_rows(o_s.reshape(db, D), DEC_PAD), bwo_bf, xs, ln_mix_g[1], ln_mix_b[1])
    y_sample = channel_mix(1, xs, ps[1])[:db].reshape(db, 1, D)

    return (y_prompt, y_sample,
            k_p.reshape(bsz, seq, HEADS, HD), v_p.reshape(bsz, seq, HEADS, HD),
            lf_p.reshape(bsz, seq, HEADS).astype(ldt),
            c_p[None].astype(sdt), n_p[None].astype(sdt), m_p[:, :, 0][None].astype(sdt),
            k_s[:db].reshape(db, 1, HEADS, HD), v_s[:db].reshape(db, 1, HEADS, HD),
            lf_s[:db].reshape(db, 1, HEADS).astype(ldt),
            c_s[None].astype(sdt), n_s[None].astype(sdt), m_s[:, 0, :HEADS][None].astype(sdt))
```

```python
import functools
import math

import jax
import jax.numpy as jnp
from jax import lax
from jax.experimental import pallas as pl
from jax.experimental.pallas import tpu as pltpu

F32 = jnp.float32
BF16 = jnp.bfloat16
I32 = jnp.int32

D = 1024
HEADS = 8
HD = 128
DEPTH = 2
PEER_HEADS = 8
PEER_NKEYS = 128
PEER_TOPK = 16
ALPHA = (2.0 * DEPTH) ** 0.25
LN_EPS = 1e-5
HEAD_NORM_EPS = 1e-6
CHUNK = 128
DEC_PAD = 128

LANES = 128
VMEM_LIMIT = 56 * 1024 * 1024

NEG_INF = float("-inf")


def _cparams(sem, vmem=None):
    return pltpu.CompilerParams(dimension_semantics=sem, vmem_limit_bytes=vmem or VMEM_LIMIT)


def _layer_norm(z, g, b):
    mu = jnp.mean(z, axis=-1, keepdims=True)
    zc = z - mu
    var = jnp.mean(zc * zc, axis=-1, keepdims=True)
    return zc * lax.rsqrt(var + LN_EPS) * g + b


def _log_sigmoid(z):
    return jnp.minimum(z, 0.0) - jnp.log1p(jnp.exp(-jnp.abs(z)))


def _split_bf16(a):
    hi = a.astype(BF16)
    lo = (a - hi.astype(F32)).astype(BF16)
    return hi, lo


def _dot(a, b):
    return jnp.dot(a, b, preferred_element_type=F32)


def _dot_nt(a, b):
    return lax.dot_general(a, b, (((1,), (1,)), ((), ())), preferred_element_type=F32)


def _dot3(x, w):
    xh, xl = _split_bf16(x)
    wh, wl = _split_bf16(w)
    return _dot(xh, wh) + _dot(xl, wh) + _dot(xh, wl)


def _mm_kernel(x_ref, w_ref, o_ref, xb_ref):
    @pl.when(pl.program_id(1) == 0)
    def _():
        xb_ref[...] = x_ref[...].astype(BF16)

    o_ref[...] = _dot(xb_ref[...], w_ref[...]).astype(o_ref.dtype)


def matmul(x, w_bf, tm=1024, tn=1024, out_dtype=F32):
    m, k = x.shape
    n = w_bf.shape[1]
    tm = min(tm, m)
    tn = min(tn, n)
    return pl.pallas_call(
        _mm_kernel,
        grid=(m // tm, n // tn),
        in_specs=[pl.BlockSpec((tm, k), lambda i, j: (i, 0)),
                  pl.BlockSpec((k, tn), lambda i, j: (0, j))],
        out_specs=pl.BlockSpec((tm, tn), lambda i, j: (i, j)),
        out_shape=jax.ShapeDtypeStruct((m, n), out_dtype),
        scratch_shapes=[pltpu.VMEM((tm, k), BF16)],
        compiler_params=_cparams(("parallel", "arbitrary")),
        name="matmul",
    )(x, w_bf)


def _gates_kernel(x_ref, w_ref, b_ref, o_ref, *, ls_start):
    z = _dot3(x_ref[...], w_ref[...]) + b_ref[...]
    col = lax.broadcasted_iota(I32, z.shape, 1)
    o_ref[...] = jnp.where(col >= ls_start, _log_sigmoid(z), z)


def gates_matmul(x, w_pad, b_pad, ls_start, tm=512):
    m, k = x.shape
    tm = min(tm, m)
    return pl.pallas_call(
        functools.partial(_gates_kernel, ls_start=ls_start),
        grid=(m // tm,),
        in_specs=[pl.BlockSpec((tm, k), lambda i: (i, 0)),
                  pl.BlockSpec((k, LANES), lambda i: (0, 0)),
                  pl.BlockSpec((1, LANES), lambda i: (0, 0))],
        out_specs=pl.BlockSpec((tm, LANES), lambda i: (i, 0)),
        out_shape=jax.ShapeDtypeStruct((m, LANES), F32),
        compiler_params=_cparams(("parallel",)),
        name="gates_matmul",
    )(x, w_pad, b_pad)


def _mm_ln_kernel(a_ref, w_ref, r_ref, g_ref, b_ref, o_ref):
    y = _dot(a_ref[...].astype(BF16), w_ref[...])
    o_ref[...] = _layer_norm(ALPHA * r_ref[...] + y, g_ref[...], b_ref[...])


def matmul_ln(a, w_bf, res, g, b, tm=512):
    m, k = a.shape
    n = w_bf.shape[1]
    tm = min(tm, m)
    return pl.pallas_call(
        _mm_ln_kernel,
        grid=(m // tm,),
        in_specs=[pl.BlockSpec((tm, k), lambda i: (i, 0)),
                  pl.BlockSpec((k, n), lambda i: (0, 0)),
                  pl.BlockSpec((tm, n), lambda i: (i, 0)),
                  pl.BlockSpec((1, n), lambda i: (0, 0)),
                  pl.BlockSpec((1, n), lambda i: (0, 0))],
        out_specs=pl.BlockSpec((tm, n), lambda i: (i, 0)),
        out_shape=jax.ShapeDtypeStruct((m, n), F32),
        compiler_params=_cparams(("parallel",)),
        name="matmul_ln",
    )(a, w_bf, res, g.reshape(1, n), b.reshape(1, n))


def _ln_ple_kernel(xm_ref, po_ref, g_ref, b_ref, p_ref, wg_ref, bg_ref, wp_ref, o_ref):
    x2 = _layer_norm(ALPHA * xm_ref[...] + po_ref[...], g_ref[...], b_ref[...])
    gate = jax.nn.sigmoid(_dot(x2.astype(BF16), wg_ref[...]) + bg_ref[...])
    emb = _dot(p_ref[...].astype(BF16), wp_ref[...])
    o_ref[...] = x2 + gate * emb


def ln_ple(xm, po, g, b, p, wg_bf, bg, wp_bf, tm=512):
    m, n = xm.shape
    kp = p.shape[1]
    tm = min(tm, m)
    row = lambda i: (i, 0)
    fixed = lambda i: (0, 0)
    return pl.pallas_call(
        _ln_ple_kernel,
        grid=(m // tm,),
        in_specs=[pl.BlockSpec((tm, n), row), pl.BlockSpec((tm, n), row),
                  pl.BlockSpec((1, n), fixed), pl.BlockSpec((1, n), fixed),
                  pl.BlockSpec((tm, kp), row),
                  pl.BlockSpec((n, n), fixed), pl.BlockSpec((1, n), fixed),
                  pl.BlockSpec((kp, n), fixed)],
        out_specs=pl.BlockSpec((tm, n), row),
        out_shape=jax.ShapeDtypeStruct((m, n), F32),
        compiler_params=_cparams(("parallel",)),
        name="ln_ple",
    )(xm, po, g.reshape(1, n), b.reshape(1, n), p, wg_bf, bg.reshape(1, n), wp_bf)


def _ln_kv_kernel(x_ref, g_ref, b_ref, wk_ref, wv_ref, wf_ref, bf_ref, k_ref, v_ref, lf_ref, kb_ref, vb_ref):
    st = _layer_norm(x_ref[...], g_ref[...], b_ref[...])
    sb = st.astype(BF16)
    k = _dot(sb, wk_ref[...])
    v = _dot(sb, wv_ref[...])
    k_ref[...] = k
    v_ref[...] = v
    kb_ref[...] = k.astype(BF16)
    vb_ref[...] = v.astype(BF16)
    lf_ref[...] = _log_sigmoid(_dot3(st, wf_ref[...]) + bf_ref[...])


def ln_kv(x, g, b, wk_bf, wv_bf, wf_pad, bf_pad, tm=512):
    m, n = x.shape
    tm = min(tm, m)
    row = lambda i: (i, 0)
    fixed = lambda i: (0, 0)
    return pl.pallas_call(
        _ln_kv_kernel,
        grid=(m // tm,),
        in_specs=[pl.BlockSpec((tm, n), row), pl.BlockSpec((1, n), fixed), pl.BlockSpec((1, n), fixed),
                  pl.BlockSpec((n, n), fixed), pl.BlockSpec((n, n), fixed),
                  pl.BlockSpec((n, LANES), fixed), pl.BlockSpec((1, LANES), fixed)],
        out_specs=[pl.BlockSpec((tm, n), row), pl.BlockSpec((tm, n), row), pl.BlockSpec((tm, LANES), row),
                   pl.BlockSpec((tm, n), row), pl.BlockSpec((tm, n), row)],
        out_shape=[jax.ShapeDtypeStruct((m, n), F32), jax.ShapeDtypeStruct((m, n), F32),
                   jax.ShapeDtypeStruct((m, LANES), F32),
                   jax.ShapeDtypeStruct((m, n), BF16), jax.ShapeDtypeStruct((m, n), BF16)],
        compiler_params=_cparams(("parallel",)),
        name="ln_kv",
    )(x, g.reshape(1, n), b.reshape(1, n), wk_bf, wv_bf, wf_pad, bf_pad)


def _head_norm_gate(hh, o_pre, gn):
    mu = jnp.mean(hh, axis=-1, keepdims=True)
    hc = hh - mu
    var = jnp.mean(hc * hc, axis=-1, keepdims=True)
    return jax.nn.sigmoid(o_pre) * (hc * lax.rsqrt(var + HEAD_NORM_EPS)) * gn


def _mlstm_prompt_kernel(q_ref, k_ref, v_ref, o_ref, g_ref, gn_ref,
                         y_ref, c_out_ref, n_out_ref, m_out_ref, c_ref, m_ref):
    c_idx = pl.program_id(1)
    L = CHUNK

    @pl.when(c_idx == 0)
    def _():
        c_ref[...] = jnp.zeros_like(c_ref)
        m_ref[...] = jnp.zeros_like(m_ref)

    row = lax.broadcasted_iota(I32, (L, L), 0)
    col = lax.broadcasted_iota(I32, (L, L), 1)
    causal = col <= row
    tril = causal.astype(F32)
    g = g_ref[...]
    cum = jnp.dot(tril, g, preferred_element_type=F32, precision=lax.Precision.HIGHEST)
    g_t = g.T
    cum_t = cum.T
    lane = lax.broadcasted_iota(I32, (L, HD), 1)
    ones_col = (lane == 0).astype(BF16)
    scale = HD ** -0.5

    heads = range(HEADS)
    sls = [slice(h * HD, (h + 1) * HD) for h in heads]
    qbs = [q_ref[:, sl].astype(BF16) for sl in sls]
    kbs = [(k_ref[:, sl] * scale).astype(BF16) for sl in sls]
    c_augs = [c_ref[h] for h in heads]
    qks = [_dot_nt(qbs[h], kbs[h]) for h in heads]
    a_inters = [_dot(qbs[h], c_augs[h].astype(BF16)) for h in heads]

    pre = []
    for h in heads:
        li_c = g[:, h:h + 1]
        b_c = cum[:, HEADS + h:HEADS + h + 1]
        m_prev = m_ref[h:h + 1, 0:1]
        dmat = jnp.where(causal, b_c - cum_t[HEADS + h:HEADS + h + 1, :] + g_t[h:h + 1, :], NEG_INF)
        b_last = b_c[L - 1:L, :]
        g_c = b_last - b_c + li_c
        pre.append((dmat, jnp.max(dmat, axis=1, keepdims=True), b_c + m_prev, g_c,
                    jnp.max(g_c, axis=0, keepdims=True), b_last + m_prev))

    stats, wide = [], []
    for h in heads:
        dmat, dmax, inter, g_c, gmax, carry = pre[h]
        m_t = jnp.maximum(inter, dmax)
        w_inter = jnp.exp(inter - m_t)
        m_new = jnp.maximum(carry, gmax)
        decay = jnp.exp(carry - m_new)
        w_c = jnp.exp(g_c - m_new)
        stats.append((m_t, w_inter, m_new, decay))
        wide.append((jnp.broadcast_to(m_t, (L, L)), jnp.broadcast_to(w_c, (L, HD))))

    ps, vws = [], []
    for h in heads:
        m_t_wide, w_c_wide = wide[h]
        ps.append((qks[h] * jnp.exp(pre[h][0] - m_t_wide)).astype(BF16))
        vws.append(jnp.concatenate([v_ref[:, sls[h]] * w_c_wide, jnp.where(lane == 0, w_c_wide, 0.0)],
                                   axis=1).astype(BF16))

    a_intras = [_dot(ps[h], jnp.concatenate([v_ref[:, sls[h]].astype(BF16), ones_col], axis=1)) for h in heads]
    upds = [lax.dot_general(kbs[h], vws[h], (((0,), (0,)), ((), ())), preferred_element_type=F32) for h in heads]

    hhs, mus = [], []
    for h in heads:
        m_t, w_inter, m_new, decay = stats[h]
        a_inter, a_intra = a_inters[h], a_intras[h]
        num = w_inter * a_inter[:, :HD] + a_intra[:, :HD]
        den = w_inter * a_inter[:, HD:HD + 1] + a_intra[:, HD:HD + 1]
        hh = num / jnp.maximum(jnp.abs(den), jnp.exp(-m_t))
        hhs.append(hh)
        mus.append(jnp.mean(hh, axis=-1, keepdims=True))
        c_ref[h] = decay * c_augs[h] + upds[h]
        m_ref[h:h + 1, :] = jnp.broadcast_to(m_new, (1, LANES))
    hcs = [hhs[h] - mus[h] for h in heads]
    variances = [jnp.mean(hc * hc, axis=-1, keepdims=True) for hc in hcs]
    for h in heads:
        hn = hcs[h] * lax.rsqrt(variances[h] + HEAD_NORM_EPS)
        y_ref[:, sls[h]] = jax.nn.sigmoid(o_ref[:, sls[h]]) * hn * gn_ref[:, sls[h]]

    @pl.when(c_idx == pl.num_programs(1) - 1)
    def _():
        for h in range(HEADS):
            c_aug = c_ref[h]
            c_out_ref[0, h] = c_aug[:, :HD]
            n_out_ref[0, h:h + 1, :] = c_aug[:, HD:].T[0:1, :]
        m_out_ref[0] = m_ref[...]


def mlstm_prompt(qkvo, gates, gn_g, bsz, seq):
    nc = seq // CHUNK
    t = bsz * seq

    def col_spec(gidx):
        return pl.BlockSpec((CHUNK, D), lambda b, c: (b * nc + c, gidx))

    return pl.pallas_call(
        _mlstm_prompt_kernel,
        grid=(bsz, nc),
        in_specs=[col_spec(0), col_spec(1), col_spec(2), col_spec(3),
                  pl.BlockSpec((CHUNK, LANES), lambda b, c: (b * nc + c, 0)),
                  pl.BlockSpec((1, D), lambda b, c: (0, 0))],
        out_specs=[pl.BlockSpec((CHUNK, D), lambda b, c: (b * nc + c, 0)),
                   pl.BlockSpec((1, HEADS, HD, HD), lambda b, c: (b, 0, 0, 0)),
                   pl.BlockSpec((1, HEADS, HD), lambda b, c: (b, 0, 0)),
                   pl.BlockSpec((1, HEADS, LANES), lambda b, c: (b, 0, 0))],
        out_shape=[jax.ShapeDtypeStruct((t, D), F32),
                   jax.ShapeDtypeStruct((bsz, HEADS, HD, HD), F32),
                   jax.ShapeDtypeStruct((bsz, HEADS, HD), F32),
                   jax.ShapeDtypeStruct((bsz, HEADS, LANES), F32)],
        scratch_shapes=[pltpu.VMEM((HEADS, HD, 2 * HD), F32), pltpu.VMEM((HEADS, LANES), F32)],
        compiler_params=_cparams(("parallel", "arbitrary")),
        name="mlstm_prompt",
    )(qkvo, qkvo, qkvo, qkvo, gates, gn_g.reshape(1, D))


def _mlstm_step_kernel(q_ref, k_ref, v_ref, o_ref, g_ref, gn_ref, c0_ref, n0_ref, m0_ref,
                       y_ref, c_out_ref, n_out_ref, m_out_ref):
    row = lax.broadcasted_iota(I32, (HD, HD), 0)
    col = lax.broadcasted_iota(I32, (HD, HD), 1)
    eye = row == col
    scale = HD ** -0.5
    g = g_ref[0]
    m0 = m0_ref[0]
    m_new_all = jnp.zeros((1, LANES), F32)
    lane = lax.broadcasted_iota(I32, (1, LANES), 1)

    def to_col(r):
        return jnp.sum(jnp.where(eye, r, 0.0), axis=1, keepdims=True)

    for h in range(HEADS):
        sl = slice(h * HD, (h + 1) * HD)
        q = q_ref[0][:, sl]
        ks = k_ref[0][:, sl] * scale
        v = v_ref[0][:, sl]
        li = g[:, h:h + 1]
        lf = g[:, HEADS + h:HEADS + h + 1]
        m_prev = m0[:, h:h + 1]
        c0 = c0_ref[0, h]
        n0 = n0_ref[0, h:h + 1, :]
        inter = lf + m_prev
        m_t = jnp.maximum(inter, li)
        w_inter = jnp.exp(inter - m_t)
        p = jnp.exp(li - m_t)
        qk = jnp.sum(q * ks, axis=1, keepdims=True) * p
        q_c = jnp.sum(to_col(q) * c0, axis=0, keepdims=True)
        q_n = jnp.sum(q * n0, axis=1, keepdims=True)
        num = w_inter * q_c + qk * v
        den = w_inter * q_n + qk
        hh = num / jnp.maximum(jnp.abs(den), jnp.exp(-m_t))
        y_ref[0, :, sl] = _head_norm_gate(hh, o_ref[0][:, sl], gn_ref[:, sl])
        decay = w_inter
        c_out_ref[0, h] = decay * c0 + to_col(p * ks) * v
        n_out_ref[0, h:h + 1, :] = decay * n0 + p * ks
        m_new_all = jnp.where(lane == h, m_t, m_new_all)
    m_out_ref[0] = m_new_all


def mlstm_step(qkvo3, gates3, gn_g, c0, n0, m0):
    nb = c0.shape[0]

    def col_spec(gidx):
        return pl.BlockSpec((1, 1, D), lambda b: (b, 0, gidx))

    return pl.pallas_call(
        _mlstm_step_kernel,
        grid=(nb,),
        in_specs=[col_spec(0), col_spec(1), col_spec(2), col_spec(3),
                  pl.BlockSpec((1, 1, LANES), lambda b: (b, 0, 0)),
                  pl.BlockSpec((1, D), lambda b: (0, 0)),
                  pl.BlockSpec((1, HEADS, HD, HD), lambda b: (b, 0, 0, 0)),
                  pl.BlockSpec((1, HEADS, HD), lambda b: (b, 0, 0)),
                  pl.BlockSpec((1, 1, HEADS), lambda b: (b, 0, 0))],
        out_specs=[pl.BlockSpec((1, 1, D), lambda b: (b, 0, 0)),
                   pl.BlockSpec((1, HEADS, HD, HD), lambda b: (b, 0, 0, 0)),
                   pl.BlockSpec((1, HEADS, HD), lambda b: (b, 0, 0)),
                   pl.BlockSpec((1, 1, LANES), lambda b: (b, 0, 0))],
        out_shape=[jax.ShapeDtypeStruct((nb, 1, D), F32),
                   jax.ShapeDtypeStruct((nb, HEADS, HD, HD), F32),
                   jax.ShapeDtypeStruct((nb, HEADS, HD), F32),
                   jax.ShapeDtypeStruct((nb, 1, LANES), F32)],
        compiler_params=_cparams(("parallel",)),
        name="mlstm_step",
    )(qkvo3, qkvo3, qkvo3, qkvo3, gates3, gn_g.reshape(1, D), c0, n0, m0.reshape(nb, 1, HEADS))


def _suffix_kernel(lf_ref, o_ref):
    s = lf_ref.shape[2]
    row = lax.broadcasted_iota(I32, (LANES, LANES), 0)
    col = lax.broadcasted_iota(I32, (LANES, LANES), 1)
    upper = (row > col).astype(F32)
    carry = jnp.zeros((HEADS, 1), F32)
    for c in reversed(range(s // LANES)):
        x = lf_ref[0, :, c * LANES:(c + 1) * LANES]
        inner = jnp.dot(x, upper, preferred_element_type=F32, precision=lax.Precision.HIGHEST)
        o_ref[0, :, c * LANES:(c + 1) * LANES] = inner + carry
        carry = carry + jnp.sum(x, axis=1, keepdims=True)


def forget_suffix(lf_t):
    bsz, nh, s = lf_t.shape
    return pl.pallas_call(
        _suffix_kernel,
        grid=(bsz,),
        in_specs=[pl.BlockSpec((1, nh, s), lambda b: (b, 0, 0))],
        out_specs=pl.BlockSpec((1, nh, s), lambda b: (b, 0, 0)),
        out_shape=jax.ShapeDtypeStruct((bsz, nh, s), F32),
        compiler_params=_cparams(("parallel",)),
        name="forget_suffix",
    )(lf_t)


def _fox_prompt_kernel(q_ref, k_ref, v_ref, sq_ref, sk_ref, o_ref, acc_ref, m_ref, l_ref, *, tq):
    i = pl.program_id(1)
    j = pl.program_id(2)
    scale = HD ** -0.5

    @pl.when(j == 0)
    def _():
        acc_ref[...] = jnp.zeros_like(acc_ref)
        m_ref[...] = jnp.full_like(m_ref, NEG_INF)
        l_ref[...] = jnp.zeros_like(l_ref)

    def update(on_diagonal):
        sq = sq_ref[...]
        sk = sk_ref[0]
        if on_diagonal:
            visible = lax.broadcasted_iota(I32, (tq, tq), 1) <= lax.broadcasted_iota(I32, (tq, tq), 0)
        for h in range(HEADS):
            sl = slice(h * HD, (h + 1) * HD)
            sqb = jnp.broadcast_to(sq[:, h:h + 1], (tq, LANES))
            s = _dot_nt(q_ref[:, sl].astype(BF16), k_ref[:, sl].astype(BF16)) * scale + sk[h:h + 1, :]
            if on_diagonal:
                s = jnp.where(visible, s, NEG_INF)
            m_prev = m_ref[h]
            m_new = jnp.maximum(m_prev, jnp.max(s, axis=1, keepdims=True) - sqb)
            alpha = jnp.exp(m_prev - m_new)
            p = jnp.exp(s - jnp.concatenate([m_new + sqb] * (tq // LANES), axis=1))
            l_ref[h] = alpha * l_ref[h] + jnp.sum(p, axis=1, keepdims=True)
            acc_ref[:, sl] = alpha * acc_ref[:, sl] + _dot(p.astype(BF16), v_ref[:, sl].astype(BF16))
            m_ref[h] = m_new

    @pl.when(j < i)
    def _():
        update(False)

    @pl.when(j == i)
    def _():
        update(True)
        for h in range(HEADS):
            sl = slice(h * HD, (h + 1) * HD)
            o_ref[:, sl] = acc_ref[:, sl] / l_ref[h]


def fox_prompt(q, k, v, suf_tok, suf_t, bsz, seq, tq=512):
    tq = min(tq, seq)
    nq = seq // tq
    t = bsz * seq
    kv_spec = pl.BlockSpec((tq, D), lambda b, i, j: (b * nq + jnp.minimum(j, i), 0))
    return pl.pallas_call(
        functools.partial(_fox_prompt_kernel, tq=tq),
        grid=(bsz, nq, nq),
        in_specs=[pl.BlockSpec((tq, D), lambda b, i, j: (b * nq + i, 0)),
                  kv_spec, kv_spec,
                  pl.BlockSpec((tq, HEADS), lambda b, i, j: (b * nq + i, 0)),
                  pl.BlockSpec((1, HEADS, tq), lambda b, i, j: (b, 0, jnp.minimum(j, i)))],
        out_specs=pl.BlockSpec((tq, D), lambda b, i, j: (b * nq + i, 0)),
        out_shape=jax.ShapeDtypeStruct((t, D), F32),
        scratch_shapes=[pltpu.VMEM((tq, D), F32), pltpu.VMEM((HEADS, tq, LANES), F32),
                        pltpu.VMEM((HEADS, tq, LANES), F32)],
        compiler_params=_cparams(("parallel", "parallel", "arbitrary")),
        name="fox_prompt",
    )(q, k, v, suf_tok, suf_t)


def _page_suffix_kernel(lf_ref, suf_ref, tot_ref):
    w = lf_ref.shape[1]
    row = lax.broadcasted_iota(I32, (w, w), 0)
    col = lax.broadcasted_iota(I32, (w, w), 1)
    same_head = (row % HEADS) == (col % HEADS)
    later = (same_head & (row > col)).astype(BF16)
    whole = same_head.astype(BF16)
    x = lf_ref[...]
    hi = x.astype(BF16)
    r1 = x - hi.astype(F32)
    mid = r1.astype(BF16)
    lo = (r1 - mid.astype(F32)).astype(BF16)
    suf_ref[...] = _dot(hi, later) + _dot(mid, later) + _dot(lo, later)
    tot_ref[...] = _dot(hi, whole) + _dot(mid, whole) + _dot(lo, whole)


def page_suffix(lf_flat, tr=512):
    r, w = lf_flat.shape
    tr = math.gcd(r, tr)
    spec = pl.BlockSpec((tr, w), lambda i: (i, 0))
    return pl.pallas_call(
        _page_suffix_kernel,
        grid=(r // tr,),
        in_specs=[spec],
        out_specs=[spec, spec],
        out_shape=[jax.ShapeDtypeStruct((r, w), F32), jax.ShapeDtypeStruct((r, w), F32)],
        compiler_params=_cparams(("parallel",)),
        name="page_suffix",
    )(lf_flat)


def _fox_decode_kernel(pt_ref, q_ref, kn_ref, vn_ref, lfn_ref, *refs, pg):
    page_refs = refs[:4 * pg]
    o_ref, acc_ref, m_ref, l_ref, r_ref = refs[4 * pg:]
    j = pl.program_id(1)
    scale = HD ** -0.5
    w = r_ref.shape[1]
    q8 = q_ref[0]

    @pl.when(j == 0)
    def _():
        s_new = jnp.sum(q8 * kn_ref[0], axis=1, keepdims=True) * scale
        m_ref[...] = jnp.broadcast_to(s_new, m_ref.shape)
        l_ref[...] = jnp.ones_like(l_ref)
        acc_ref[...] = vn_ref[0]
        r_ref[...] = lfn_ref[0]

    run = r_ref[...]
    bias, ks, vs = [], [], []
    for i in range(pg):
        k_ref, v_ref, suf_ref, tot_ref = page_refs[4 * i:4 * i + 4]
        bias.append(suf_ref[0] + run)
        run = run + tot_ref[0]
        ks.append(k_ref[0].reshape(w, HD).astype(BF16))
        vs.append(v_ref[0].reshape(w, HD).astype(BF16))
    r_ref[...] = run
    own = (lax.broadcasted_iota(I32, (HEADS, pg * w), 1) % HEADS) == lax.broadcasted_iota(I32, (HEADS, pg * w), 0)
    s = _dot_nt(q8.astype(BF16), jnp.concatenate(ks, axis=0)) * scale + jnp.concatenate(bias, axis=1)
    s = jnp.where(own, s, NEG_INF)
    m_prev = m_ref[:, 0:1]
    m_new = jnp.maximum(m_prev, jnp.max(s, axis=1, keepdims=True))
    alpha = jnp.exp(m_prev - m_new)
    p = jnp.exp(s - m_new)
    l_ref[...] = alpha * l_ref[...] + jnp.sum(p, axis=1, keepdims=True)
    acc_ref[...] = alpha * acc_ref[...] + _dot(p.astype(BF16), jnp.concatenate(vs, axis=0))
    m_ref[...] = jnp.broadcast_to(m_new, m_ref.shape)

    @pl.when(j == pl.num_programs(1) - 1)
    def _():
        o_ref[0] = acc_ref[...] / l_ref[...]


def fox_decode(q, k, v, lf_new, cache_k, cache_v, cache_logf, page_table, pages_per_step=8):
    nb, n_pages = page_table.shape
    n_pool, page = cache_k.shape[:2]
    w = page * HEADS
    pg = math.gcd(n_pages, pages_per_step)
    suf_in, tot = page_suffix(cache_logf.reshape(n_pool, w))
    suf_in = suf_in.reshape(n_pool, 1, w)
    tot = tot.reshape(n_pool, 1, w)
    lfn = jnp.tile(lf_new, (1, page)).reshape(nb, 1, w)
    tok = lambda b, j, pt: (b, 0, 0)
    in_specs = [pl.BlockSpec((1, HEADS, HD), tok), pl.BlockSpec((1, HEADS, HD), tok), pl.BlockSpec((1, HEADS, HD), tok),
                pl.BlockSpec((1, 1, w), tok)]
    operands = []
    for i in range(pg):
        sel4 = lambda b, j, pt, i=i: (pt[b, n_pages - 1 - (j * pg + i)], 0, 0, 0)
        sel3 = lambda b, j, pt, i=i: (pt[b, n_pages - 1 - (j * pg + i)], 0, 0)
        in_specs += [pl.BlockSpec((1, page, HEADS, HD), sel4), pl.BlockSpec((1, page, HEADS, HD), sel4),
                     pl.BlockSpec((1, 1, w), sel3), pl.BlockSpec((1, 1, w), sel3)]
        operands += [cache_k, cache_v, suf_in, tot]
    grid_spec = pltpu.PrefetchScalarGridSpec(
        num_scalar_prefetch=1,
        grid=(nb, n_pages // pg),
        in_specs=in_specs,
        out_specs=pl.BlockSpec((1, HEADS, HD), tok),
        scratch_shapes=[pltpu.VMEM((HEADS, HD), F32), pltpu.VMEM((HEADS, LANES), F32),
                        pltpu.VMEM((HEADS, LANES), F32), pltpu.VMEM((1, w), F32)],
    )
    return pl.pallas_call(
        functools.partial(_fox_decode_kernel, pg=pg),
        grid_spec=grid_spec,
        out_shape=jax.ShapeDtypeStruct((nb, HEADS, HD), F32),
        compiler_params=_cparams(("parallel", "arbitrary")),
        name="fox_decode",
    )(page_table, q, k, v, lfn, *operands)


PICKS = PEER_HEADS * PEER_TOPK
EXPERT_BITS = 14
PAIR_ROWS = 8
TILE_ROWS = 2 * PAIR_ROWS
GROWS = PICKS * TILE_ROWS


def _take_topk(s, tie, k, put):
    big = 3.0e38
    for r in range(k):
        m = jnp.max(s, axis=0, keepdims=True)
        pos = jnp.min(jnp.where(s == m, tie, big), axis=0, keepdims=True)
        put(r, m, pos)
        s = jnp.where(tie == pos, NEG_INF, s)


def _take_topk_keys(s, k, put):
    g = s.shape[0] // 8
    base = lax.broadcasted_iota(I32, (8, s.shape[1]), 0).astype(F32)
    vals = [s[8 * v:8 * v + 8, :] for v in range(g)]
    ids = [base + float(8 * v) for v in range(g)]
    for rnd in range(g):
        for i in range(rnd % 2, g - 1, 2):
            swap = vals[i + 1] > vals[i]
            vals[i], vals[i + 1] = jnp.where(swap, vals[i + 1], vals[i]), jnp.where(swap, vals[i], vals[i + 1])
            ids[i], ids[i + 1] = jnp.where(swap, ids[i + 1], ids[i]), jnp.where(swap, ids[i], ids[i + 1])
    big = 3.0e38
    for r in range(k):
        m = jnp.max(vals[0], axis=0, keepdims=True)
        pos = jnp.min(jnp.where(vals[0] == m, ids[0], big), axis=0, keepdims=True)
        put(r, m, pos)
        taken = ids[0] == pos
        last = min(g, k) - 1 - r
        for v in range(last):
            vals[v] = jnp.where(taken, vals[v + 1], vals[v])
            ids[v] = jnp.where(taken, ids[v + 1], ids[v])
        if last >= 0:
            vals[last] = jnp.where(taken, NEG_INF, vals[last])


def _candidate_bins(kk):
    full, small = [], []
    for a in range(kk):
        nb = kk // (a + 1)
        b0 = 0
        while nb - b0 >= 8:
            full.append([(a, b0, 0, 8)])
            b0 += 8
        if nb > b0:
            small.append((a, b0, nb - b0))
    bins = []
    for a, b0, rows in sorted(small, key=lambda piece: -piece[2]):
        for group in bins:
            used = group[-1][2] + group[-1][3]
            if used + rows <= 8:
                group.append((a, b0, used, rows))
                break
        else:
            bins.append([(a, b0, 0, rows)])
    return full + bins


def _peer_select_kernel(x_ref, wq_ref, sk_ref, idx_ref, off_ref, g_ref, sc_ref, sv_ref, si_ref, e_ref, gg_ref):
    tb = x_ref.shape[0]
    kk = PEER_TOPK
    nk = PEER_NKEYS
    lw = LANES
    q = _dot(x_ref[...].astype(BF16), wq_ref[...]).astype(BF16)
    sub8 = lax.broadcasted_iota(I32, (8, lw), 0)
    sub8f = sub8.astype(F32)
    bins = _candidate_bins(kk)

    for h in range(PEER_HEADS):
        for c in range(2):
            qc = q[:, (2 * h + c) * nk:(2 * h + c + 1) * nk]
            sc_ref[c] = _dot_nt(sk_ref[c], qc)
        for part in range(tb // lw):
            ls = slice(part * lw, (part + 1) * lw)
            for c in range(2):
                def put1(r, m, pos, c=c):
                    sv_ref[c, r:r + 1, ls] = m
                    si_ref[c, r:r + 1, ls] = pos

                _take_topk_keys(sc_ref[c, :, ls], kk, put1)

            cand, ckey = [], []
            for pieces in bins:
                va = ia = vb = ib = flat = None
                used = 0
                for a, b0, start, rows in pieces:
                    ra = jnp.broadcast_to(sv_ref[0, a:a + 1, ls], (8, lw))
                    ri = jnp.broadcast_to(si_ref[0, a:a + 1, ls], (8, lw))
                    rb = sv_ref[1, b0:b0 + 8, ls]
                    rj = si_ref[1, b0:b0 + 8, ls]
                    rf = sub8f + float(a * kk + b0 - start)
                    if start:
                        rb = pltpu.roll(rb, start, axis=0)
                        rj = pltpu.roll(rj, start, axis=0)
                        here = sub8 >= start
                        va, ia, vb = jnp.where(here, ra, va), jnp.where(here, ri, ia), jnp.where(here, rb, vb)
                        ib, flat = jnp.where(here, rj, ib), jnp.where(here, rf, flat)
                    else:
                        va, ia, vb, ib, flat = ra, ri, rb, rj, rf
                    used = start + rows
                c8 = va + vb
                cand.append(c8 if used == 8 else jnp.where(sub8 < used, c8, NEG_INF))
                ckey.append(flat * float(2 ** EXPERT_BITS) + (ia * float(nk) + ib))
            cand = jnp.concatenate(cand, axis=0)
            ckey = jnp.concatenate(ckey, axis=0)

            def put2(r, m, pos, h=h):
                gg_ref[h * kk + r:h * kk + r + 1, ls] = m
                e_ref[h * kk + r:h * kk + r + 1, ls] = pos

            _take_topk(cand, ckey, kk, put2)
            cv = gg_ref[h * kk:(h + 1) * kk, ls]
            ex = jnp.exp(cv - cv[0:1, :])
            gg_ref[h * kk:(h + 1) * kk, ls] = ex / jnp.sum(ex, axis=0, keepdims=True)

    expert = e_ref[...].T.astype(I32) & (2 ** EXPERT_BITS - 1)
    idx_ref[...] = expert
    off_ref[...] = (expert >> 1) * PAIR_ROWS
    g_ref[...] = gg_ref[...].T


def peer_select(x, wq_bf, sk_bf, tb=256):
    m, n = x.shape
    nq = wq_bf.shape[1]
    tb = min(tb, m)
    out_spec = pl.BlockSpec((tb, PICKS), lambda i: (i, 0))
    return pl.pallas_call(
        _peer_select_kernel,
        grid=(m // tb,),
        in_specs=[pl.BlockSpec((tb, n), lambda i: (i, 0)),
                  pl.BlockSpec((n, nq), lambda i: (0, 0)),
                  pl.BlockSpec((2, PEER_NKEYS, PEER_NKEYS), lambda i: (0, 0, 0))],
        out_specs=[out_spec, out_spec, out_spec],
        out_shape=[jax.ShapeDtypeStruct((m, PICKS), I32), jax.ShapeDtypeStruct((m, PICKS), I32),
                   jax.ShapeDtypeStruct((m, PICKS), F32)],
        scratch_shapes=[pltpu.VMEM((2, PEER_NKEYS, tb), F32),
                        pltpu.VMEM((2, PEER_TOPK, tb), F32), pltpu.VMEM((2, PEER_TOPK, tb), F32),
                        pltpu.VMEM((PICKS, tb), F32), pltpu.VMEM((PICKS, tb), F32)],
        compiler_params=_cparams(("parallel",)),
        name="peer_select",
    )(x, wq_bf, sk_bf)


def _diag_mask():
    sub = lax.broadcasted_iota(I32, (8, GROWS), 0)
    lane = lax.broadcasted_iota(I32, (8, GROWS), 1)
    return (lane % TILE_ROWS) // 2 == sub


def _gather_pairs(off_smem, tab_ref, t):
    tiles = []
    for p in range(PICKS):
        off = pl.multiple_of(off_smem[t, p], PAIR_ROWS)
        tiles.append(pltpu.bitcast(tab_ref[pl.ds(off, PAIR_ROWS), :], BF16))
    return jnp.concatenate(tiles, axis=0)


TOKEN_UNROLL = 64


def _peer_up_kernel(off_smem, x_ref, idxv_ref, gate_ref, tab_ref, ecomp_ref, w_ref, a_ref):
    tbu = x_ref.shape[0]
    mdiag = _diag_mask()

    def body(t, carry):
        g = _gather_pairs(off_smem, tab_ref, t)
        xr = x_ref[pl.ds(t, 1), :]
        x8 = jnp.concatenate([xr[:, s * LANES:(s + 1) * LANES] for s in range(D // LANES)], axis=0)
        res = _dot_nt(x8.astype(BF16), g)
        a_ref[pl.ds(t, 1), :] = jnp.sum(jnp.where(mdiag, res, 0.0), axis=0, keepdims=True)
        return carry

    lax.fori_loop(0, tbu, body, 0, unroll=TOKEN_UNROLL)
    hi, lo = _split_bf16(a_ref[...])
    both = _dot(hi, ecomp_ref[...]) + _dot(lo, ecomp_ref[...])
    odd = (idxv_ref[...] & 1) == 1
    a = jnp.where(odd, both[:, PICKS:], both[:, :PICKS])
    w_ref[...] = gate_ref[...] * (0.5 * a * (1.0 + lax.erf(a * (2.0 ** -0.5))))


def _peer_down_kernel(off_smem, w_ref, idxv_ref, tab_ref, eexp_ref, o_ref, wx_ref):
    tbu = w_ref.shape[0]
    mdiag = _diag_mask()
    w = w_ref[...]
    odd = (idxv_ref[...] & 1) == 1
    w01 = jnp.concatenate([jnp.where(odd, 0.0, w), jnp.where(odd, w, 0.0)], axis=1).astype(BF16)
    wx_ref[...] = _dot(w01, eexp_ref[...])

    def body(t, carry):
        g = _gather_pairs(off_smem, tab_ref, t)
        wm = jnp.where(mdiag, wx_ref[pl.ds(t, 1), :], 0.0).astype(BF16)
        o_ref[t] = _dot(wm, g)
        return carry

    lax.fori_loop(0, tbu, body, 0, unroll=TOKEN_UNROLL)


def _expand_matrix():
    r = lax.broadcasted_iota(I32, (2 * PICKS, GROWS), 0)
    k = lax.broadcasted_iota(I32, (2 * PICKS, GROWS), 1)
    return ((k // TILE_ROWS == r % PICKS) & (k % 2 == r // PICKS)).astype(BF16)


def _table_spec(tab):
    return pl.BlockSpec(tab.shape, lambda i: (0, 0), pipeline_mode=pl.Buffered(1))


def peer_up(off, idx, x, gate, tab, tbu=128):
    m = x.shape[0]
    tbu = min(tbu, m)
    ecomp = _expand_matrix().T
    return pl.pallas_call(
        _peer_up_kernel,
        grid=(m // tbu,),
        in_specs=[pl.BlockSpec((tbu, PICKS), lambda i: (i, 0), memory_space=pltpu.SMEM),
                  pl.BlockSpec((tbu, D), lambda i: (i, 0)),
                  pl.BlockSpec((tbu, PICKS), lambda i: (i, 0)),
                  pl.BlockSpec((tbu, PICKS), lambda i: (i, 0)),
                  _table_spec(tab),
                  pl.BlockSpec((GROWS, 2 * PICKS), lambda i: (0, 0))],
        out_specs=pl.BlockSpec((tbu, PICKS), lambda i: (i, 0)),
        out_shape=jax.ShapeDtypeStruct((m, PICKS), F32),
        scratch_shapes=[pltpu.VMEM((tbu, GROWS), F32)],
        compiler_params=_cparams(("arbitrary",)),
        name="peer_up",
    )(off, x, idx, gate, tab, ecomp)


def peer_down(off, idx, w, tab, tbu=128):
    m = w.shape[0]
    tbu = min(tbu, m)
    eexp = _expand_matrix()
    return pl.pallas_call(
        _peer_down_kernel,
        grid=(m // tbu,),
        in_specs=[pl.BlockSpec((tbu, PICKS), lambda i: (i, 0), memory_space=pltpu.SMEM),
                  pl.BlockSpec((tbu, PICKS), lambda i: (i, 0)),
                  pl.BlockSpec((tbu, PICKS), lambda i: (i, 0)),
                  _table_spec(tab),
                  pl.BlockSpec((2 * PICKS, GROWS), lambda i: (0, 0))],
        out_specs=pl.BlockSpec((tbu, 8, LANES), lambda i: (i, 0, 0)),
        out_shape=jax.ShapeDtypeStruct((m, 8, LANES), F32),
        scratch_shapes=[pltpu.VMEM((tbu, GROWS), F32)],
        compiler_params=_cparams(("arbitrary",)),
        name="peer_down",
    )(off, w, idx, tab, eexp).reshape(m, D)


def _pack_table(tab):
    e = tab.shape[0]
    bits = lax.bitcast_convert_type(tab.astype(BF16), jnp.uint16).astype(jnp.uint32).reshape(e // 2, 2, D)
    words = bits[:, 0, :] | (bits[:, 1, :] << 16)
    return words.reshape(e // 2 * PAIR_ROWS, LANES)


def peer(x, wq_bf, sk_bf, u_tab, v_tab):
    idx, off, gate = peer_select(x, wq_bf, sk_bf)
    w = peer_up(off, idx, x, gate, u_tab)
    return peer_down(off, idx, w, v_tab)


def _pad_cols(w, n=LANES):
    return jnp.pad(w, ((0, 0), (0, n - w.shape[1])))


def _pad_rows(a, rows):
    return jnp.pad(a, ((0, rows - a.shape[0]),) + ((0, 0),) * (a.ndim - 1))


def kernel(x_prompt, x_sample, cache_k, cache_v, cache_logf, state_C, state_n, state_m, page_table, p_prompt, p_sample, a_w_in, a_b_gate, a_gn_g, a_w_out, kv_ln_g, kv_ln_b, kv_w, kv_b_f, b_w_q, b_w_o, ln_mix_g, ln_mix_b, ln_ffn_g, ln_ffn_b, peer_w_q, peer_subkeys, peer_u, peer_v, ple_w_p, ple_w_g, ple_b_g):
    bsz, seq, _ = x_prompt.shape
    db = x_sample.shape[0]
    n_pool, page = cache_k.shape[:2]
    sdt = state_C.dtype
    ldt = cache_logf.dtype

    w_in_bf = a_w_in[0][:, :4 * D].astype(BF16)
    w_gate = _pad_cols(a_w_in[0][:, 4 * D:])
    b_gate = _pad_cols(a_b_gate[0].reshape(1, -1))
    w_out_bf = a_w_out[0].astype(BF16)
    wk_bf = kv_w[:, :D].astype(BF16)
    wv_bf = kv_w[:, D:2 * D].astype(BF16)
    wf = _pad_cols(kv_w[:, 2 * D:])
    bf = _pad_cols(kv_b_f.reshape(1, -1))
    bwq_bf = b_w_q[0].astype(BF16)
    bwo_bf = b_w_o[0].astype(BF16)
    pwq_bf = peer_w_q.astype(BF16)
    sk_bf = peer_subkeys.astype(BF16)
    u_tabs = [_pack_table(peer_u[i]) for i in range(DEPTH)]
    v_tabs = [_pack_table(peer_v[i]) for i in range(DEPTH)]
    wg_bf = ple_w_g.astype(BF16)
    wp_bf = ple_w_p.astype(BF16)

    def channel_mix(i, x_mid, p):
        po = peer(x_mid, pwq_bf[i], sk_bf[i], u_tabs[i], v_tabs[i])
        return ln_ple(x_mid, po, ln_ffn_g[i], ln_ffn_b[i], p, wg_bf[i], ple_b_g[i], wp_bf[i])

    def shared_kv(x):
        k, v, lfp, k_bf, v_bf = ln_kv(x, kv_ln_g, kv_ln_b, wk_bf, wv_bf, wf, bf)
        return k, v, lfp[:, :HEADS], k_bf, v_bf

    t = bsz * seq
    x = x_prompt.reshape(t, D)
    p = p_prompt.reshape(DEPTH, t, -1)
    qkvo = matmul(x, w_in_bf)
    gates = gates_matmul(x, w_gate, b_gate, HEADS)
    y_pre, c_p, n_p, m_p = mlstm_prompt(qkvo, gates, a_gn_g[0], bsz, seq)
    x = matmul_ln(y_pre, w_out_bf, x, ln_mix_g[0], ln_mix_b[0])
    x = channel_mix(0, x, p[0])

    k_p, v_p, lf_p, k_pb, v_pb = shared_kv(x)
    lf_t = lf_p.reshape(bsz, seq, HEADS).transpose(0, 2, 1)
    suf_t = forget_suffix(lf_t)
    suf_tok = suf_t.transpose(0, 2, 1).reshape(t, HEADS)
    q = matmul(x, bwq_bf, out_dtype=BF16)
    o = fox_prompt(q, k_pb, v_pb, suf_tok, suf_t, bsz, seq)
    x = matmul_ln(o, bwo_bf, x, ln_mix_g[1], ln_mix_b[1])
    y_prompt = channel_mix(1, x, p[1]).reshape(bsz, seq, D)

    xs = _pad_rows(x_sample.reshape(db, D), DEC_PAD)
    ps = jnp.pad(p_sample.reshape(DEPTH, db, -1), ((0, 0), (0, DEC_PAD - db), (0, 0)))
    qkvo_s = matmul(xs, w_in_bf)
    gates_s = gates_matmul(xs, w_gate, b_gate, HEADS)
    y_s, c_s, n_s, m_s = mlstm_step(qkvo_s[:db].reshape(db, 1, 4 * D), gates_s[:db].reshape(db, 1, LANES),
                                    a_gn_g[0], state_C[0].astype(F32), state_n[0].astype(F32),
                                    state_m[0].astype(F32))
    xs = matmul_ln(_pad_rows(y_s.reshape(db, D), DEC_PAD), w_out_bf, xs, ln_mix_g[0], ln_mix_b[0])
    xs = channel_mix(0, xs, ps[0])

    k_s, v_s, lf_s, _, _ = shared_kv(xs)
    q_s = matmul(xs, bwq_bf)
    o_s = fox_decode(q_s[:db].reshape(db, HEADS, HD), k_s[:db].reshape(db, HEADS, HD), v_s[:db].reshape(db, HEADS, HD),
                     lf_s[:db], cache_k, cache_v, cache_logf.astype(F32), page_table)
    xs = matmul_ln(_pad_rows(o_s.reshape(db, D), DEC_PAD), bwo_bf, xs, ln_mix_g[1], ln_mix_b[1])
    y_sample = channel_mix(1, xs, ps[1])[:db].reshape(db, 1, D)

    return (y_prompt, y_sample,
            k_p.reshape(bsz, seq, HEADS, HD), v_p.reshape(bsz, seq, HEADS, HD),
            lf_p.reshape(bsz, seq, HEADS).astype(ldt),
            c_p[None].astype(sdt), n_p[None].astype(sdt), m_p[:, :, 0][None].astype(sdt),
            k_s[:db].reshape(db, 1, HEADS, HD), v_s[:db].reshape(db, 1, HEADS, HD),
            lf_s[:db].reshape(db, 1, HEADS).astype(ldt),
            c_s[None].astype(sdt), n_s[None].astype(sdt), m_s[:, 0, :HEADS][None].astype(sdt))
```
